```python
import math
import jax, jax.numpy as jnp
from jax import lax
import numpy as np

D_MODEL = 1024
BATCH = 2
SEQ = 8192
DEPTH = 2
DEC_BATCH = 128
DEC_SEQ = 4
PAST_LEN = 16384
PAGE_SIZE = 128

N_HEADS = 4
HEAD_DIM = 64
N_BRANCH = 4
BRANCH_WIDTH = N_HEADS * HEAD_DIM
DIFF_QK = 32
MLA_Q_RANK = 256
MLA_KV_RANK = 128
MLA_NOPE = 64
MLA_ROPE = 32
IDX_HEADS = 8
IDX_DIM = 32
TOPK_MAX = 256
D_PLE = 256
D_FF = 2816
N_EXPERTS = 8
TOP_K_EXPERTS = 2
D_FF_EXPERT = 3584
ROPE_THETA = 10000.0
RMS_EPS = 1e-6
Q_BLOCK = 128

IN_SPLITS = (
    N_HEADS * HEAD_DIM, HEAD_DIM, HEAD_DIM,
    N_HEADS * 2 * DIFF_QK, 2 * DIFF_QK, HEAD_DIM,
    MLA_Q_RANK, MLA_KV_RANK, MLA_ROPE,
    N_HEADS * HEAD_DIM, HEAD_DIM, HEAD_DIM,
    IDX_HEADS * IDX_DIM, IDX_DIM, IDX_HEADS,
)
IN_COLS = sum(IN_SPLITS)
SB_ROW = 2 * HEAD_DIM
DIFF_ROW = 2 * DIFF_QK + HEAD_DIM
MLA_ROW = MLA_KV_RANK + MLA_ROPE
DSA_ROW = 2 * HEAD_DIM
IDX_ROW = IDX_DIM

kernel_name = 'hybrid_sb_diff_mla_dsa_step'


def rmsnorm(x, g):
    xf = x.astype(jnp.float32)
    y = xf * lax.rsqrt(jnp.mean(xf * xf, axis=-1, keepdims=True) + RMS_EPS)
    return (y * g.astype(jnp.float32)).astype(x.dtype)


def rope(x, pos):
    half = x.shape[-1] // 2
    inv_freq = ROPE_THETA ** (-jnp.arange(half, dtype=jnp.float32) / half)
    ang = pos.astype(jnp.float32)[:, None] * inv_freq[None, :]
    ang = ang.reshape((ang.shape[0],) + (1,) * (x.ndim - 3) + (half,))
    cos, sin = jnp.cos(ang), jnp.sin(ang)
    x1, x2 = jnp.split(x.astype(jnp.float32), 2, axis=-1)
    return jnp.concatenate([x1 * cos - x2 * sin, x1 * sin + x2 * cos], axis=-1).astype(x.dtype)


def attend(fn, pos, *qs):
    B, T = qs[0].shape[:2]
    if T <= Q_BLOCK:
        return fn(pos, *qs)
    nb = T // Q_BLOCK
    blocks = tuple(jnp.moveaxis(q.reshape((B, nb, Q_BLOCK) + q.shape[2:]), 1, 0) for q in qs)
    out = lax.map(lambda args: fn(*args), (pos.reshape(nb, Q_BLOCK),) + blocks)
    return jnp.moveaxis(out, 0, 1).reshape((B, T) + out.shape[3:])


def gather_pages(cache, layer, page_table):
    rows = cache[layer, page_table]
    return rows.reshape(rows.shape[0], -1, rows.shape[-1])


def gather_selected(cache, layer, page_table, new_rows, sel):
    past_len = page_table.shape[1] * PAGE_SIZE
    t_new = new_rows.shape[1]
    past_idx = jnp.minimum(sel, past_len - 1)
    page = jax.vmap(lambda pt, s: pt[s])(page_table, past_idx // PAGE_SIZE)
    rows_past = cache[layer, page, past_idx % PAGE_SIZE]
    rows_new = jax.vmap(lambda r, s: r[s])(new_rows, jnp.clip(sel - past_len, 0, t_new - 1))
    return jnp.where((sel < past_len)[..., None], rows_past, rows_new)


def stick_breaking(q_pos, q, k, v, k_pos):
    z = jnp.einsum('bthd,bsd->bhts', q, k).astype(jnp.float32) * (q.shape[-1] ** -0.5)
    visible = k_pos[None, :] < q_pos[:, None]
    log_keep = jnp.where(visible, -jax.nn.softplus(z), 0.0)
    log_between = lax.cumsum(log_keep, axis=3, reverse=True) - log_keep
    w = jnp.where(visible, jnp.exp(jax.nn.log_sigmoid(z) + log_between), 0.0)
    return jnp.einsum('bhts,bsd->bthd', w.astype(v.dtype), v)


def diff_attention(q_pos, q, k, v, k_pos, lam):
    s = jnp.einsum('bthcd,bscd->bchts', q, k).astype(jnp.float32) * (DIFF_QK ** -0.5)
    visible = k_pos[None, :] <= q_pos[:, None]
    p = jax.nn.softmax(jnp.where(visible, s, -jnp.inf), axis=-1)
    attn = p[:, 0] - lam * p[:, 1]
    return jnp.einsum('bhts,bsd->bthd', attn.astype(v.dtype), v)


def mla_attention(q_pos, q_lat, q_pe, c_kv, k_pe, k_pos):
    s = (jnp.einsum('bthc,bsc->bhts', q_lat, c_kv).astype(jnp.float32)
         + jnp.einsum('bthr,bsr->bhts', q_pe, k_pe).astype(jnp.float32)) * ((MLA_NOPE + MLA_ROPE) ** -0.5)
    visible = k_pos[None, :] <= q_pos[:, None]
    p = jax.nn.softmax(jnp.where(visible, s, -jnp.inf), axis=-1)
    return jnp.einsum('bhts,bsc->bthc', p.astype(c_kv.dtype), c_kv)


def dsa_attention(q_pos, q, iq, iw, k_idx, k_pos, gather_rows):
    rel = jax.nn.relu(jnp.einsum('bthd,bsd->bths', iq, k_idx).astype(jnp.float32))
    score = jnp.einsum('bths,bth->bts', rel, iw.astype(jnp.float32))
    visible = k_pos[None, :] <= q_pos[:, None]
    score = jnp.where(visible[None], score, -jnp.inf)
    n_sel = min(TOPK_MAX, k_idx.shape[1] // 4)
    top, sel = lax.top_k(score, n_sel)
    valid = jnp.isfinite(top)
    rows = gather_rows(sel)
    ks, vs = rows[..., :HEAD_DIM], rows[..., HEAD_DIM:]
    s = jnp.einsum('bthd,btkd->bhtk', q, ks).astype(jnp.float32) * (HEAD_DIM ** -0.5)
    p = jax.nn.softmax(jnp.where(valid[:, None], s, -jnp.inf), axis=-1)
    return jnp.einsum('bhtk,btkd->bthd', p.astype(vs.dtype), vs)


def swiglu(x, wg, wu, wd):
    return jnp.einsum('btf,fd->btd', jax.nn.silu(jnp.einsum('btd,df->btf', x, wg)) * jnp.einsum('btd,df->btf', x, wu), wd)


def moe_swiglu(x, w_router, wg, wu, wd):
    logits = jnp.einsum('btd,de->bte', x, w_router).astype(jnp.float32)
    top_val, top_idx = lax.top_k(logits, TOP_K_EXPERTS)
    top_w = jax.nn.softmax(top_val, axis=-1)
    combine = jnp.sum(jax.nn.one_hot(top_idx, N_EXPERTS, dtype=jnp.float32) * top_w[..., None], axis=-2)
    out = jnp.zeros_like(x)
    for e in range(N_EXPERTS):
        out = out + combine[..., e:e + 1].astype(x.dtype) * swiglu(x, wg[e], wu[e], wd[e])
    return out


def token_mixers(a, pos, W, i, past):
    B, T, _ = a.shape
    H = N_HEADS
    z = jnp.einsum('btd,dc->btc', a, W['w_in'][i])
    split_at = [int(v) for v in np.cumsum(IN_SPLITS)[:-1]]
    (sb_q, sb_k, sb_v, df_q, df_k, df_v, m_cq, m_ckv, m_kpe,
     ds_q, ds_k, ds_v, ix_q, ix_k, ix_w) = jnp.split(z, split_at, axis=-1)

    sb_q = sb_q.reshape(B, T, H, HEAD_DIM)
    sb_row = jnp.concatenate([sb_k, sb_v], axis=-1)
    df_q = rope(rmsnorm(df_q.reshape(B, T, H, 2, DIFF_QK), W['diff_q_norm'][i]), pos)
    df_k = rope(rmsnorm(df_k.reshape(B, T, 2, DIFF_QK), W['diff_k_norm'][i]), pos)
    diff_row = jnp.concatenate([df_k.reshape(B, T, 2 * DIFF_QK), df_v], axis=-1)
    q_full = jnp.einsum('btr,rhe->bthe', rmsnorm(m_cq, W['mla_q_norm_a'][i]), W['w_mla_uq'][i])
    q_nope = rmsnorm(q_full[..., :MLA_NOPE], W['mla_qn_nope'][i])
    q_pe = rope(rmsnorm(q_full[..., MLA_NOPE:], W['mla_qn_pe'][i]), pos)
    q_lat = jnp.einsum('bthe,che->bthc', q_nope, W['w_mla_uk'][i])
    mla_row = jnp.concatenate([rmsnorm(m_ckv, W['mla_kv_norm'][i]),
                               rope(rmsnorm(m_kpe, W['mla_kn_pe'][i]), pos)], axis=-1)
    ds_q = rope(rmsnorm(ds_q.reshape(B, T, H, HEAD_DIM), W['dsa_q_norm'][i]), pos)
    ds_row = jnp.concatenate([rope(rmsnorm(ds_k, W['dsa_k_norm'][i]), pos), ds_v], axis=-1)
    ix_q = rope(ix_q.reshape(B, T, IDX_HEADS, IDX_DIM), pos)
    idx_row = rope(ix_k, pos)

    if past is None:
        key_pos = pos
        sb_keys, diff_keys, mla_keys, idx_keys = sb_row, diff_row, mla_row, idx_row
        gather_dsa = lambda sel: jax.vmap(lambda r, s: r[s])(ds_row, sel)
    else:
        c_sb, c_diff, c_mla, c_dsa, c_idx, page_table = past
        past_len = page_table.shape[1] * PAGE_SIZE
        key_pos = jnp.arange(past_len + T, dtype=jnp.int32)
        sb_keys = jnp.concatenate([gather_pages(c_sb, i, page_table), sb_row], axis=1)
        diff_keys = jnp.concatenate([gather_pages(c_diff, i, page_table), diff_row], axis=1)
        mla_keys = jnp.concatenate([gather_pages(c_mla, i, page_table), mla_row], axis=1)
        idx_keys = jnp.concatenate([gather_pages(c_idx, i, page_table), idx_row], axis=1)
        gather_dsa = lambda sel: gather_selected(c_dsa, i, page_table, ds_row, sel)

    sb_k_all, sb_v_all = sb_keys[..., :HEAD_DIM], sb_keys[..., HEAD_DIM:]
    sb_out = attend(lambda qp, q: stick_breaking(qp, q, sb_k_all, sb_v_all, key_pos), pos, sb_q)

    lam_init = 0.8 - 0.6 * math.exp(-0.3 * i)
    lam = (jnp.exp(jnp.sum(W['diff_lq1'][i] * W['diff_lk1'][i]).astype(jnp.float32))
           - jnp.exp(jnp.sum(W['diff_lq2'][i] * W['diff_lk2'][i]).astype(jnp.float32)) + lam_init)
    df_k_all = diff_keys[..., :2 * DIFF_QK].reshape(diff_keys.shape[0], diff_keys.shape[1], 2, DIFF_QK)
    df_v_all = diff_keys[..., 2 * DIFF_QK:]
    diff_out = attend(lambda qp, q: diff_attention(qp, q, df_k_all, df_v_all, key_pos, lam), pos, df_q)
    diff_out = rmsnorm(diff_out, W['diff_subln'][i]) * (1.0 - lam_init)

    c_all, kpe_all = mla_keys[..., :MLA_KV_RANK], mla_keys[..., MLA_KV_RANK:]
    mla_lat = attend(lambda qp, ql, qr: mla_attention(qp, ql, qr, c_all, kpe_all, key_pos), pos, q_lat, q_pe)
    mla_out = jnp.einsum('bthc,chv->bthv', mla_lat, W['w_mla_uv'][i])

    dsa_out = attend(lambda qp, q, iq, iw: dsa_attention(qp, q, iq, iw, idx_keys, key_pos, gather_dsa),
                     pos, ds_q, ix_q, ix_w)

    outs = jnp.stack([sb_out.reshape(B, T, BRANCH_WIDTH), diff_out.reshape(B, T, BRANCH_WIDTH),
                      mla_out.reshape(B, T, BRANCH_WIDTH), dsa_out.reshape(B, T, BRANCH_WIDTH)], axis=2)
    y = jnp.einsum('btnc,ncd->btnd', outs, W['w_branch'][i])
    gates = jax.nn.sigmoid(jnp.einsum('btd,de->bte', a, W['w_gate'][i])).reshape(B, T, N_BRANCH, D_MODEL)
    merged = jnp.sum(gates * y, axis=2)
    mix = jnp.einsum('btd,de->bte', merged, W['w_out'][i])
    return mix, (sb_row, diff_row, mla_row, ds_row, idx_row)


def run_group(x, p, W, past):
    B, T, _ = x.shape
    past_len = 0 if past is None else past[-1].shape[1] * PAGE_SIZE
    pos = past_len + jnp.arange(T, dtype=jnp.int32)
    h = x
    rows = []
    for i in range(DEPTH):
        mix, new_rows = token_mixers(rmsnorm(h, W['norm_mix'][i]), pos, W, i, past)
        h = h + mix
        c = rmsnorm(h, W['norm_ffn'][i])
        j = i // 2
        if i % 2 == 0:
            h = h + swiglu(c, W['w_ff_gate'][j], W['w_ff_up'][j], W['w_ff_down'][j])
        else:
            h = h + moe_swiglu(c, W['w_router'][j], W['w_moe_gate'][j], W['w_moe_up'][j], W['w_moe_down'][j])
        ple_gate = jax.nn.sigmoid(jnp.einsum('btd,de->bte', rmsnorm(h, W['norm_ple'][i]), W['w_ple_gate'][i]))
        h = h + ple_gate * jnp.einsum('btp,pd->btd', p[i], W['w_ple'][i])
        rows.append(new_rows)
    new_state = tuple(jnp.stack([r[n] for r in rows], axis=0) for n in range(5))
    return h, new_state


def setup_inputs(seed: int = 0) -> dict:
    key = jax.random.key(seed)
    keys = jax.random.split(key, 64)
    counter = iter(range(64))

    def normal(shape, scale):
        return scale * jax.random.normal(keys[next(counter)], shape, dtype=jnp.float32)

    def gain(shape):
        return 1.0 + 0.05 * jax.random.normal(keys[next(counter)], shape, dtype=jnp.float32)

    n_pages = PAST_LEN // PAGE_SIZE
    n_used = DEC_BATCH * n_pages
    n_pool = n_used + n_used // 4
    n_dense = (DEPTH + 1) // 2
    n_moe = DEPTH // 2
    x_prompt = normal((BATCH, SEQ, D_MODEL), 1.0)
    x_sample = normal((DEC_BATCH, DEC_SEQ, D_MODEL), 1.0)
    cache_sb = normal((DEPTH, n_pool, PAGE_SIZE, SB_ROW), 1.0)
    cache_diff = normal((DEPTH, n_pool, PAGE_SIZE, DIFF_ROW), 1.0)
    cache_mla = normal((DEPTH, n_pool, PAGE_SIZE, MLA_ROW), 1.0)
    cache_dsa_kv = normal((DEPTH, n_pool, PAGE_SIZE, DSA_ROW), 1.0)
    cache_dsa_idx = normal((DEPTH, n_pool, PAGE_SIZE, IDX_ROW), 1.0)
    page_table = jax.random.permutation(keys[next(counter)], n_pool)[:n_used].reshape(DEC_BATCH, n_pages).astype(jnp.int32)
    p_prompt = normal((DEPTH, BATCH, SEQ, D_PLE), 1.0)
    p_sample = normal((DEPTH, DEC_BATCH, DEC_SEQ, D_PLE), 1.0)
    return {
        'x_prompt': x_prompt, 'x_sample': x_sample,
        'cache_sb': cache_sb, 'cache_diff': cache_diff, 'cache_mla': cache_mla,
        'cache_dsa_kv': cache_dsa_kv, 'cache_dsa_idx': cache_dsa_idx,
        'page_table': page_table, 'p_prompt': p_prompt, 'p_sample': p_sample,
        'norm_mix': gain((DEPTH, D_MODEL)),
        'w_in': normal((DEPTH, D_MODEL, IN_COLS), D_MODEL ** -0.5),
        'diff_q_norm': gain((DEPTH, DIFF_QK)),
        'diff_k_norm': gain((DEPTH, DIFF_QK)),
        'diff_lq1': normal((DEPTH, DIFF_QK), 0.1),
        'diff_lk1': normal((DEPTH, DIFF_QK), 0.1),
        'diff_lq2': normal((DEPTH, DIFF_QK), 0.1),
        'diff_lk2': normal((DEPTH, DIFF_QK), 0.1),
        'diff_subln': gain((DEPTH, HEAD_DIM)),
        'mla_q_norm_a': gain((DEPTH, MLA_Q_RANK)),
        'mla_kv_norm': gain((DEPTH, MLA_KV_RANK)),
        'w_mla_uq': normal((DEPTH, MLA_Q_RANK, N_HEADS, MLA_NOPE + MLA_ROPE), MLA_Q_RANK ** -0.5),
        'w_mla_uk': normal((DEPTH, MLA_KV_RANK, N_HEADS, MLA_NOPE), MLA_KV_RANK ** -0.5),
        'w_mla_uv': normal((DEPTH, MLA_KV_RANK, N_HEADS, HEAD_DIM), MLA_KV_RANK ** -0.5),
        'mla_qn_nope': gain((DEPTH, MLA_NOPE)),
        'mla_qn_pe': gain((DEPTH, MLA_ROPE)),
        'mla_kn_pe': gain((DEPTH, MLA_ROPE)),
        'dsa_q_norm': gain((DEPTH, HEAD_DIM)),
        'dsa_k_norm': gain((DEPTH, HEAD_DIM)),
        'w_branch': normal((DEPTH, N_BRANCH, BRANCH_WIDTH, D_MODEL), BRANCH_WIDTH ** -0.5),
        'w_gate': normal((DEPTH, D_MODEL, N_BRANCH * D_MODEL), D_MODEL ** -0.5),
        'w_out': normal((DEPTH, D_MODEL, D_MODEL), D_MODEL ** -0.5),
        'norm_ffn': gain((DEPTH, D_MODEL)),
        'w_ff_gate': normal((n_dense, D_MODEL, D_FF), D_MODEL ** -0.5),
        'w_ff_up': normal((n_dense, D_MODEL, D_FF), D_MODEL ** -0.5),
        'w_ff_down': normal((n_dense, D_FF, D_MODEL), D_FF ** -0.5),
        'w_router': normal((n_moe, D_MODEL, N_EXPERTS), D_MODEL ** -0.5),
        'w_moe_gate': normal((n_moe, N_EXPERTS, D_MODEL, D_FF_EXPERT), D_MODEL ** -0.5),
        'w_moe_up': normal((n_moe, N_EXPERTS, D_MODEL, D_FF_EXPERT), D_MODEL ** -0.5),
        'w_moe_down': normal((n_moe, N_EXPERTS, D_FF_EXPERT, D_MODEL), D_FF_EXPERT ** -0.5),
        'norm_ple': gain((DEPTH, D_MODEL)),
        'w_ple_gate': normal((DEPTH, D_MODEL, D_MODEL), D_MODEL ** -0.5),
        'w_ple': normal((DEPTH, D_PLE, D_MODEL), D_PLE ** -0.5),
    }


def reference(x_prompt, x_sample, cache_sb, cache_diff, cache_mla, cache_dsa_kv, cache_dsa_idx,
              page_table, p_prompt, p_sample, norm_mix, w_in, diff_q_norm, diff_k_norm,
              diff_lq1, diff_lk1, diff_lq2, diff_lk2, diff_subln, mla_q_norm_a, mla_kv_norm,
              w_mla_uq, w_mla_uk, w_mla_uv, mla_qn_nope, mla_qn_pe, mla_kn_pe, dsa_q_norm,
              dsa_k_norm, w_branch, w_gate, w_out, norm_ffn, w_ff_gate, w_ff_up, w_ff_down,
              w_router, w_moe_gate, w_moe_up, w_moe_down, norm_ple, w_ple_gate, w_ple):
    W = dict(norm_mix=norm_mix, w_in=w_in, diff_q_norm=diff_q_norm, diff_k_norm=diff_k_norm,
             diff_lq1=diff_lq1, diff_lk1=diff_lk1, diff_lq2=diff_lq2, diff_lk2=diff_lk2,
             diff_subln=diff_subln, mla_q_norm_a=mla_q_norm_a, mla_kv_norm=mla_kv_norm,
             w_mla_uq=w_mla_uq, w_mla_uk=w_mla_uk, w_mla_uv=w_mla_uv, mla_qn_nope=mla_qn_nope,
             mla_qn_pe=mla_qn_pe, mla_kn_pe=mla_kn_pe, dsa_q_norm=dsa_q_norm, dsa_k_norm=dsa_k_norm,
             w_branch=w_branch, w_gate=w_gate, w_out=w_out, norm_ffn=norm_ffn,
             w_ff_gate=w_ff_gate, w_ff_up=w_ff_up, w_ff_down=w_ff_down, w_router=w_router,
             w_moe_gate=w_moe_gate, w_moe_up=w_moe_up, w_moe_down=w_moe_down,
             norm_ple=norm_ple, w_ple_gate=w_ple_gate, w_ple=w_ple)
    past = (cache_sb, cache_diff, cache_mla, cache_dsa_kv, cache_dsa_idx, page_table)
    y_prompt, (sb_p, diff_p, mla_p, dsa_p, idx_p) = run_group(x_prompt, p_prompt, W, None)
    y_sample, (sb_s, diff_s, mla_s, dsa_s, idx_s) = run_group(x_sample, p_sample, W, past)
    return (y_prompt, y_sample, sb_p, sb_s, diff_p, diff_s, mla_p, mla_s, dsa_p, dsa_s, idx_p, idx_s)
```

```python
import functools
import math

import numpy as np
import jax
import jax.numpy as jnp
from jax import lax
from jax.experimental import pallas as pl
from jax.experimental.pallas import tpu as pltpu

F32, BF16, I32 = jnp.float32, jnp.bfloat16, jnp.int32

N_HEADS = 4
HEAD_DIM = 64
N_BRANCH = 4
BRANCH_WIDTH = N_HEADS * HEAD_DIM
DIFF_QK = 32
MLA_Q_RANK = 256
MLA_KV_RANK = 128
MLA_NOPE = 64
MLA_ROPE = 32
IDX_HEADS = 8
IDX_DIM = 32
TOPK_MAX = 256
N_EXPERTS = 8
ROPE_THETA = 10000.0
RMS_EPS = 1e-6
PAGE = 128
IN_SPLITS = (
    N_HEADS * HEAD_DIM, HEAD_DIM, HEAD_DIM,
    N_HEADS * 2 * DIFF_QK, 2 * DIFF_QK, HEAD_DIM,
    MLA_Q_RANK, MLA_KV_RANK, MLA_ROPE,
    N_HEADS * HEAD_DIM, HEAD_DIM, HEAD_DIM,
    IDX_HEADS * IDX_DIM, IDX_DIM, IDX_HEADS,
)
IN_COLS = sum(IN_SPLITS)

LANES = 128
SUBLANES = 8
VMEM_LIMIT = 52 * 1024 * 1024
NEG = -1e30
INT_MIN = -2 ** 31
IDX_BIG = 2 ** 30
SAMPLE_TP = SUBLANES
PAGES_PER_GROUP = 16


def _cparams(*sem):
    return pltpu.CompilerParams(dimension_semantics=sem, vmem_limit_bytes=VMEM_LIMIT)


def _dot(a, b):
    return jnp.dot(a, b, preferred_element_type=F32)


def _dot_nt(a, b):
    return lax.dot_general(a, b, (((1,), (1,)), ((), ())), preferred_element_type=F32)


def _pick(n, cands):
    for c in cands:
        if n % c == 0:
            return c
    return n


def _largest_divisor(n, maxd):
    for d in range(min(n, maxd), 0, -1):
        if n % d == 0:
            return d
    return 1


def _rms(x, g):
    return x * lax.rsqrt(jnp.mean(x * x, axis=-1, keepdims=True) + RMS_EPS) * g


def _heads_to_rows(x, width, starts):
    return jnp.concatenate([x[:, s:s + width] for s in starts], axis=0)


def _rows_to_heads(x, n, t):
    return jnp.concatenate([x[h * t:(h + 1) * t] for h in range(n)], axis=1)


def _tri(n):
    return (lax.broadcasted_iota(I32, (n, n), 0) >= lax.broadcasted_iota(I32, (n, n), 1)).astype(BF16)


def _sb_block(q, kv, carry, acc, tri, vis):
    k = kv[:, :HEAD_DIM].astype(BF16)
    v = kv[:, HEAD_DIM:].astype(BF16)
    z = _dot_nt(q, k)
    t = jnp.log1p(jnp.exp(-jnp.abs(z)))
    log_keep = -(jnp.maximum(z, 0.0) + t)
    log_beta = jnp.minimum(z, 0.0) - t
    if vis is not None:
        log_keep = jnp.where(vis, log_keep, 0.0)
    hi = log_keep.astype(BF16)
    lo = (log_keep - hi.astype(F32)).astype(BF16)
    incl = _dot(hi, tri) + _dot(lo, tri)
    w = jnp.exp(log_beta + (incl - log_keep) + carry)
    if vis is not None:
        w = jnp.where(vis, w, 0.0)
    acc = acc + _dot(w.astype(BF16), v)
    return carry + incl[:, :1], acc


def _sm_block(s, v, m, l, acc):
    m_new = jnp.maximum(m, jnp.max(s, axis=1, keepdims=True))
    alpha = jnp.exp(m - m_new)
    p = jnp.exp(s - m_new)
    l = alpha * l + jnp.sum(p, axis=1, keepdims=True)
    acc = alpha * acc + _dot(p.astype(BF16), v)
    return m_new, l, acc


def _sm_init(rows, width):
    return (jnp.full((rows, 1), NEG, F32), jnp.zeros((rows, 1), F32), jnp.zeros((rows, width), F32))


def _diff_queries(x):
    return [_heads_to_rows(x, DIFF_QK, [h * 2 * DIFF_QK + c * DIFF_QK for h in range(N_HEADS)]).astype(BF16)
            for c in range(2)]


def _diff_step(qs, kv, st, vis):
    v = kv[:, 2 * DIFF_QK:].astype(BF16)
    out = []
    for c in range(2):
        k = kv[:, c * DIFF_QK:(c + 1) * DIFF_QK].astype(BF16)
        s = _dot_nt(qs[c], k) * (DIFF_QK ** -0.5)
        if vis is not None:
            s = jnp.where(vis, s, NEG)
        out.extend(_sm_block(s, v, *st[3 * c:3 * c + 3]))
    return tuple(out)


def _diff_finish(st, lam, g, post_scale, t):
    o = st[2] * (1.0 / st[1]) - lam * (st[5] * (1.0 / st[4]))
    o = _rms(o, g) * post_scale
    return _rows_to_heads(o, N_HEADS, t)


def _mla_queries(xl, xp):
    ql = _heads_to_rows(xl, MLA_KV_RANK, [h * MLA_KV_RANK for h in range(N_HEADS)]).astype(BF16)
    qp = _heads_to_rows(xp, MLA_ROPE, [h * MLA_ROPE for h in range(N_HEADS)]).astype(BF16)
    return ql, qp


def _mla_step(ql, qp, kv, st, vis):
    c = kv[:, :MLA_KV_RANK].astype(BF16)
    kp = kv[:, MLA_KV_RANK:].astype(BF16)
    s = (_dot_nt(ql, c) + _dot_nt(qp, kp)) * ((MLA_NOPE + MLA_ROPE) ** -0.5)
    if vis is not None:
        s = jnp.where(vis, s, NEG)
    return _sm_block(s, c, *st)


def _mla_finish(st, wuv_ref, t):
    lat = (st[2] * (1.0 / st[1])).astype(BF16)
    return jnp.concatenate([_dot(lat[h * t:(h + 1) * t], wuv_ref[h]) for h in range(N_HEADS)], axis=1)


def _idx_queries(x):
    return _heads_to_rows(x, IDX_DIM, [h * IDX_DIM for h in range(IDX_HEADS)]).astype(BF16)


def _idx_score(iq, iw, kx, t):
    r = jnp.maximum(_dot_nt(iq, kx.astype(BF16)), 0.0)
    sc = r[:t] * iw[:, 0:1]
    for h in range(1, IDX_HEADS):
        sc = sc + r[h * t:(h + 1) * t] * iw[:, h:h + 1]
    return sc


def _sortable(x):
    x = jnp.where(x == 0.0, 0.0, x)
    b = pltpu.bitcast(x, I32)
    return b ^ ((b >> 31) & jnp.int32(0x7FFFFFFF))


def _count(ref, nchunks, cw, pred):
    rows = ref.shape[0]

    def body(c, acc):
        base = pl.multiple_of(c * cw, LANES)
        m = pred(ref[:, pl.ds(base, cw)], base).astype(I32)
        for j in range(cw // LANES):
            acc = acc + m[:, j * LANES:(j + 1) * LANES]
        return acc

    acc = lax.fori_loop(0, nchunks, body, jnp.zeros((rows, LANES), I32))
    return jnp.sum(acc, axis=1, keepdims=True)


def _select_threshold(ref, nchunks, cw, k, idx_bits, real_rows):
    rows = ref.shape[0]

    def bit_body(i, u):
        cand_u = u | lax.shift_left(jnp.int32(1), 31 - i)
        cand = cand_u ^ jnp.int32(INT_MIN)
        cnt = _count(ref, nchunks, cw, lambda blk, base: blk >= cand)
        return jnp.where(cnt >= k, cand_u, u)

    u = lax.fori_loop(0, 32, bit_body, jnp.zeros((rows, 1), I32))
    thr = u ^ jnp.int32(INT_MIN)
    cnt_gt = _count(ref, nchunks, cw, lambda blk, base: blk > thr)
    cnt_ge = _count(ref, nchunks, cw, lambda blk, base: blk >= thr)
    need = k - cnt_gt
    tie = (cnt_ge > k) & (thr > INT_MIN)
    if real_rows is not None:
        tie = tie & real_rows

    def tie_path():
        def jbody(i, j):
            cand = j | lax.shift_left(jnp.int32(1), idx_bits - 1 - i)

            def pred(blk, base):
                kid = base + lax.broadcasted_iota(I32, (1, cw), 1)
                return (blk == thr) & (kid < cand)

            cnt = _count(ref, nchunks, cw, pred)
            return jnp.where(cnt <= need, cand, j)

        return lax.fori_loop(0, idx_bits, jbody, jnp.zeros((rows, 1), I32))

    jl = lax.cond(jnp.max(tie.astype(I32)) > 0, tie_path, lambda: jnp.zeros((rows, 1), I32))
    jl = jnp.where(tie, jl, jnp.where(thr > INT_MIN, IDX_BIG, 0))
    return thr, jl


def _dsa_bias(key, thr, jl, base, heads):
    n = key.shape[1]
    kid = base + lax.broadcasted_iota(I32, (1, n), 1)
    sel = (key > thr) | ((key == thr) & (kid < jl))
    bias = jnp.where(sel, 0.0, NEG)
    return jnp.concatenate([bias] * heads, axis=0)


def _kblock(ref, kb, tq):
    return ref[0, pl.ds(pl.multiple_of(kb * tq, tq), tq), :]


def _causal_vis(tq, heads, strict):
    row = lax.broadcasted_iota(I32, (heads * tq, 1), 0) & (tq - 1)
    col = lax.broadcasted_iota(I32, (1, tq), 1)
    return (col < row) if strict else (col <= row)


def _sb_prompt_kernel(q_ref, kv_ref, o_ref, *, tq):
    qi = pl.program_id(1)
    rows = N_HEADS * tq
    q = (_heads_to_rows(q_ref[0], HEAD_DIM, [h * HEAD_DIM for h in range(N_HEADS)])
         * (HEAD_DIM ** -0.5)).astype(BF16)
    tri = _tri(tq)
    st = (jnp.zeros((rows, 1), F32), jnp.zeros((rows, HEAD_DIM), F32))
    st = _sb_block(q, _kblock(kv_ref, qi, tq), *st, tri, _causal_vis(tq, N_HEADS, True))
    st = lax.fori_loop(0, qi, lambda i, c: _sb_block(q, _kblock(kv_ref, qi - 1 - i, tq), *c, tri, None), st)
    o_ref[0] = _rows_to_heads(st[1], N_HEADS, tq)


def _diff_prompt_kernel(lam_ref, q_ref, kv_ref, g_ref, o_ref, *, tq, post_scale):
    qi = pl.program_id(1)
    rows = N_HEADS * tq
    qs = _diff_queries(q_ref[0])
    st = _sm_init(rows, HEAD_DIM) * 2
    st = lax.fori_loop(0, qi, lambda i, c: _diff_step(qs, _kblock(kv_ref, i, tq), c, None), st)
    st = _diff_step(qs, _kblock(kv_ref, qi, tq), st, _causal_vis(tq, N_HEADS, False))
    o_ref[0] = _diff_finish(st, lam_ref[0], g_ref[...], post_scale, tq)


def _mla_prompt_kernel(ql_ref, qp_ref, kv_ref, wuv_ref, o_ref, *, tq):
    qi = pl.program_id(1)
    rows = N_HEADS * tq
    ql, qp = _mla_queries(ql_ref[0], qp_ref[0])
    st = _sm_init(rows, MLA_KV_RANK)
    st = lax.fori_loop(0, qi, lambda i, c: _mla_step(ql, qp, _kblock(kv_ref, i, tq), c, None), st)
    st = _mla_step(ql, qp, _kblock(kv_ref, qi, tq), st, _causal_vis(tq, N_HEADS, False))
    o_ref[0] = _mla_finish(st, wuv_ref, tq)


def _dsa_prompt_kernel(iq_ref, iw_ref, kidx_ref, q_ref, kv_ref, o_ref, key_scr, *, tq, n_sel, idx_bits):
    qi = pl.program_id(1)
    iq = _idx_queries(iq_ref[0])
    iw = iw_ref[0]

    def score(kb, vis):
        key = _sortable(_idx_score(iq, iw, _kblock(kidx_ref, kb, tq), tq))
        if vis is not None:
            key = jnp.where(vis, key, INT_MIN)
        key_scr[:, pl.ds(pl.multiple_of(kb * tq, tq), tq)] = key

    def score_body(i, c):
        score(i, None)
        return c

    lax.fori_loop(0, qi, score_body, 0)
    score(qi, _causal_vis(tq, 1, False))
    thr, jl = _select_threshold(key_scr, qi + 1, tq, n_sel, idx_bits, None)

    q = (_heads_to_rows(q_ref[0], HEAD_DIM, [h * HEAD_DIM for h in range(N_HEADS)])
         * (HEAD_DIM ** -0.5)).astype(BF16)

    def att(kb, st):
        kv = _kblock(kv_ref, kb, tq)
        base = pl.multiple_of(kb * tq, tq)
        bias = _dsa_bias(key_scr[:, pl.ds(base, tq)], thr, jl, base, N_HEADS)
        s = _dot_nt(q, kv[:, :HEAD_DIM].astype(BF16)) + bias
        return _sm_block(s, kv[:, HEAD_DIM:].astype(BF16), *st)

    st = lax.fori_loop(0, qi + 1, att, _sm_init(N_HEADS * tq, HEAD_DIM))
    o_ref[0] = _rows_to_heads(st[2] * (1.0 / st[1]), N_HEADS, tq)


def _prompt_specs(tq, widths, t_total, kv_widths):
    qs = [pl.BlockSpec((1, tq, w), lambda b, i: (b, i, 0)) for w in widths]
    ks = [pl.BlockSpec((1, t_total, w), lambda b, i: (b, 0, 0)) for w in kv_widths]
    return qs, ks


def _prompt_tq(t):
    return _pick(t, (128, 64, 32, 16, 8))


def sb_prompt(q, kv):
    b, t, _ = q.shape
    tq = _prompt_tq(t)
    qs, ks = _prompt_specs(tq, [BRANCH_WIDTH], t, [kv.shape[-1]])
    return pl.pallas_call(
        functools.partial(_sb_prompt_kernel, tq=tq),
        grid=(b, t // tq), in_specs=qs + ks,
        out_specs=pl.BlockSpec((1, tq, BRANCH_WIDTH), lambda b, i: (b, i, 0)),
        out_shape=jax.ShapeDtypeStruct((b, t, BRANCH_WIDTH), F32),
        compiler_params=_cparams("parallel", "arbitrary"), name="sb_prompt",
    )(q, kv)


def diff_prompt(lam, q, kv, g, post_scale):
    b, t, _ = q.shape
    tq = _prompt_tq(t)
    qs, ks = _prompt_specs(tq, [BRANCH_WIDTH], t, [kv.shape[-1]])
    return pl.pallas_call(
        functools.partial(_diff_prompt_kernel, tq=tq, post_scale=post_scale),
        grid=(b, t // tq),
        in_specs=[pl.BlockSpec(memory_space=pltpu.SMEM)] + qs + ks
        + [pl.BlockSpec((1, HEAD_DIM), lambda b, i: (0, 0))],
        out_specs=pl.BlockSpec((1, tq, BRANCH_WIDTH), lambda b, i: (b, i, 0)),
        out_shape=jax.ShapeDtypeStruct((b, t, BRANCH_WIDTH), F32),
        compiler_params=_cparams("parallel", "arbitrary"), name="diff_prompt",
    )(lam, q, kv, g)


def mla_prompt(ql, qp, kv, wuv):
    b, t, _ = ql.shape
    tq = _prompt_tq(t)
    qs, ks = _prompt_specs(tq, [ql.shape[-1], qp.shape[-1]], t, [kv.shape[-1]])
    return pl.pallas_call(
        functools.partial(_mla_prompt_kernel, tq=tq),
        grid=(b, t // tq),
        in_specs=qs + ks + [pl.BlockSpec(wuv.shape, lambda b, i: (0, 0, 0))],
        out_specs=pl.BlockSpec((1, tq, BRANCH_WIDTH), lambda b, i: (b, i, 0)),
        out_shape=jax.ShapeDtypeStruct((b, t, BRANCH_WIDTH), F32),
        compiler_params=_cparams("parallel", "arbitrary"), name="mla_prompt",
    )(ql, qp, kv, wuv)


def dsa_prompt(iq, iw, kidx, q, kv):
    b, t, _ = q.shape
    tq = _prompt_tq(t)
    n_sel = min(TOPK_MAX, t // 4)
    qs, ks = _prompt_specs(tq, [iq.shape[-1], iw.shape[-1]], t, [kidx.shape[-1]])
    qs2, ks2 = _prompt_specs(tq, [BRANCH_WIDTH], t, [kv.shape[-1]])
    return pl.pallas_call(
        functools.partial(_dsa_prompt_kernel, tq=tq, n_sel=n_sel, idx_bits=int(t).bit_length()),
        grid=(b, t // tq),
        in_specs=qs + ks + qs2 + ks2,
        out_specs=pl.BlockSpec((1, tq, BRANCH_WIDTH), lambda b, i: (b, i, 0)),
        out_shape=jax.ShapeDtypeStruct((b, t, BRANCH_WIDTH), F32),
        scratch_shapes=[pltpu.VMEM((tq, t), I32)],
        compiler_params=_cparams("parallel", "arbitrary"), name="dsa_prompt",
    )(iq, iw, kidx, q, kv)


class _Pager:
    def __init__(self, pt_ref, cache_ref, buf, sem, layer, n_pages, group):
        self.pt, self.cache, self.buf, self.sem = pt_ref, cache_ref, buf, sem
        self.layer, self.n_pages, self.group = layer, n_pages, group
        self.n_groups = n_pages // group

    def _copy(self, page, slot, p):
        return pltpu.make_async_copy(self.cache.at[self.layer, page],
                                     self.buf.at[slot, pl.ds(p * PAGE, PAGE)], self.sem.at[slot])

    def first_page(self, g):
        return self.n_pages - (g + 1) * self.group

    def start(self, b, g, slot):
        base = self.first_page(g)
        for p in range(self.group):
            self._copy(self.pt[b, base + p], slot, p).start()

    def wait(self, slot):
        for p in range(self.group):
            self._copy(0, slot, p).wait()

    def sweep(self, body, init):
        b = pl.program_id(0)
        nb = pl.num_programs(0)
        ng = self.n_groups

        @pl.when(b == 0)
        def _():
            self.start(0, 0, 0)

        def gbody(g, carry):
            slot = (b * ng + g) % 2

            @pl.when(g + 1 < ng)
            def _():
                self.start(b, g + 1, 1 - slot)

            @pl.when((g + 1 == ng) & (b + 1 < nb))
            def _():
                self.start(b + 1, 0, 1 - slot)

            self.wait(slot)
            return body(g, self.buf.at[slot], carry)

        return lax.fori_loop(0, ng, gbody, init)


def _sample_vis(heads, strict):
    row = lax.broadcasted_iota(I32, (heads * SAMPLE_TP, 1), 0) & (SAMPLE_TP - 1)
    col = lax.broadcasted_iota(I32, (1, PAGE), 1)
    return (col < row) if strict else (col <= row)


def _sb_sample_kernel(pt_ref, q_ref, new_ref, cache_ref, o_ref, buf, sem, *, layer, n_pages, group):
    tp = SAMPLE_TP
    rows = N_HEADS * tp
    pager = _Pager(pt_ref, cache_ref, buf, sem, layer, n_pages, group)
    q = (_heads_to_rows(q_ref[0], HEAD_DIM, [h * HEAD_DIM for h in range(N_HEADS)])
         * (HEAD_DIM ** -0.5)).astype(BF16)
    ck = min(2 * PAGE, group * PAGE)
    tri = _tri(ck)
    st = (jnp.zeros((rows, 1), F32), jnp.zeros((rows, HEAD_DIM), F32))
    st = _sb_block(q, new_ref[0], *st, _tri(PAGE), _sample_vis(N_HEADS, True))

    def body(g, rows_ref, st):
        for c in reversed(range(group * PAGE // ck)):
            st = _sb_block(q, rows_ref[pl.ds(c * ck, ck), :], *st, tri, None)
        return st

    st = pager.sweep(body, st)
    o_ref[0] = _rows_to_heads(st[1], N_HEADS, tp)


def _diff_sample_kernel(pt_ref, lam_ref, q_ref, new_ref, g_ref, cache_ref, o_ref, buf, sem, *,
                        layer, n_pages, group, post_scale):
    tp = SAMPLE_TP
    pager = _Pager(pt_ref, cache_ref, buf, sem, layer, n_pages, group)
    qs = _diff_queries(q_ref[0])
    st = _sm_init(N_HEADS * tp, HEAD_DIM) * 2
    st = _diff_step(qs, new_ref[0], st, _sample_vis(N_HEADS, False))
    st = pager.sweep(lambda g, rows_ref, st: _diff_step(qs, rows_ref[...], st, None), st)
    o_ref[0] = _diff_finish(st, lam_ref[0], g_ref[...], post_scale, tp)


def _mla_sample_kernel(pt_ref, ql_ref, qp_ref, new_ref, wuv_ref, cache_ref, o_ref, buf, sem, *,
                       layer, n_pages, group):
    tp = SAMPLE_TP
    pager = _Pager(pt_ref, cache_ref, buf, sem, layer, n_pages, group)
    ql, qp = _mla_queries(ql_ref[0], qp_ref[0])
    st = _sm_init(N_HEADS * tp, MLA_KV_RANK)
    st = _mla_step(ql, qp, new_ref[0], st, _sample_vis(N_HEADS, False))
    st = pager.sweep(lambda g, rows_ref, st: _mla_step(ql, qp, rows_ref[...], st, None), st)
    o_ref[0] = _mla_finish(st, wuv_ref, tp)


def _idx_sample_kernel(pt_ref, iq_ref, iw_ref, new_ref, cache_ref, key_ref, buf, sem, *,
                       layer, n_pages, group):
    tp = SAMPLE_TP
    pager = _Pager(pt_ref, cache_ref, buf, sem, layer, n_pages, group)
    iq = _idx_queries(iq_ref[0])
    iw = iw_ref[0]
    key = _sortable(_idx_score(iq, iw, new_ref[0], tp))
    key_ref[0, :, pl.ds(n_pages * PAGE, PAGE)] = jnp.where(_sample_vis(1, False), key, INT_MIN)

    def body(g, rows_ref, c):
        base = pl.multiple_of(pager.first_page(g) * PAGE, PAGE)
        key_ref[0, :, pl.ds(base, group * PAGE)] = _sortable(_idx_score(iq, iw, rows_ref[...], tp))
        return c

    pager.sweep(body, 0)


def _thr_sample_kernel(key_ref, thr_ref, jl_ref, *, nchunks, cw, n_sel, idx_bits, t_real):
    rows = key_ref.shape[0]
    real = (lax.broadcasted_iota(I32, (rows, 1), 0) & (SAMPLE_TP - 1)) < t_real
    thr, jl = _select_threshold(key_ref, nchunks, cw, n_sel, idx_bits, real)
    thr_ref[...] = jnp.broadcast_to(thr, thr_ref.shape)
    jl_ref[...] = jnp.broadcast_to(jl, jl_ref.shape)


def _dsa_sample_kernel(pt_ref, q_ref, key_ref, thr_ref, jl_ref, new_ref, cache_ref, o_ref, buf, sem, *,
                       layer, n_pages, group):
    tp = SAMPLE_TP
    pager = _Pager(pt_ref, cache_ref, buf, sem, layer, n_pages, group)
    q = (_heads_to_rows(q_ref[0], HEAD_DIM, [h * HEAD_DIM for h in range(N_HEADS)])
         * (HEAD_DIM ** -0.5)).astype(BF16)
    thr = thr_ref[0][:, :1]
    jl = jl_ref[0][:, :1]

    def att(kv, base, n, st):
        bias = _dsa_bias(key_ref[0, :, pl.ds(base, n)], thr, jl, base, N_HEADS)
        s = _dot_nt(q, kv[:, :HEAD_DIM].astype(BF16)) + bias
        return _sm_block(s, kv[:, HEAD_DIM:].astype(BF16), *st)

    st = att(new_ref[0], n_pages * PAGE, PAGE, _sm_init(N_HEADS * tp, HEAD_DIM))

    def body(g, rows_ref, st):
        base = pl.multiple_of(pager.first_page(g) * PAGE, PAGE)
        return att(rows_ref[...], base, group * PAGE, st)

    st = pager.sweep(body, st)
    o_ref[0] = _rows_to_heads(st[2] * (1.0 / st[1]), N_HEADS, tp)


def _paged_call(kern, page_table, pre, cache, out_width, out_dtype, name):
    nb = page_table.shape[0]
    specs = []
    for kind, a in pre:
        if kind == "smem":
            specs.append(pl.BlockSpec(memory_space=pltpu.SMEM))
        elif kind == "seq":
            specs.append(pl.BlockSpec((1,) + a.shape[1:], lambda b, pt: (b, 0, 0)))
        else:
            specs.append(pl.BlockSpec(a.shape, lambda b, pt, n=a.ndim: (0,) * n))
    specs.append(pl.BlockSpec(memory_space=pl.ANY))
    pre = [a for _, a in pre]
    group = _largest_divisor(page_table.shape[1], PAGES_PER_GROUP)
    return pl.pallas_call(
        functools.partial(kern, n_pages=page_table.shape[1], group=group),
        grid_spec=pltpu.PrefetchScalarGridSpec(
            num_scalar_prefetch=1, grid=(nb,), in_specs=specs,
            out_specs=pl.BlockSpec((1, SAMPLE_TP, out_width), lambda b, pt: (b, 0, 0)),
            scratch_shapes=[pltpu.VMEM((2, group * PAGE, cache.shape[-1]), F32),
                            pltpu.SemaphoreType.DMA((2,))]),
        out_shape=jax.ShapeDtypeStruct((nb, SAMPLE_TP, out_width), out_dtype),
        compiler_params=_cparams("arbitrary"), name=name,
    )(page_table, *pre, cache)


def _pad_rows(x, n):
    return jnp.pad(x, ((0, 0), (0, n - x.shape[1]), (0, 0)))


def sb_sample(page_table, q, new, cache, layer):
    kern = functools.partial(_sb_sample_kernel, layer=layer)
    return _paged_call(kern, page_table, [("seq", q), ("seq", new)], cache, BRANCH_WIDTH, F32, "sb_sample")


def diff_sample(page_table, lam, q, new, g, cache, layer, post_scale):
    kern = functools.partial(_diff_sample_kernel, layer=layer, post_scale=post_scale)
    pre = [("smem", lam), ("seq", q), ("seq", new), ("full", g)]
    return _paged_call(kern, page_table, pre, cache, BRANCH_WIDTH, F32, "diff_sample")


def mla_sample(page_table, ql, qp, new, wuv, cache, layer):
    kern = functools.partial(_mla_sample_kernel, layer=layer)
    pre = [("seq", ql), ("seq", qp), ("seq", new), ("full", wuv)]
    return _paged_call(kern, page_table, pre, cache, BRANCH_WIDTH, F32, "mla_sample")


def dsa_sample(page_table, iq, iw, new_idx, cache_idx, q, new_kv, cache_kv, layer, t_real):
    nb, n_pages = page_table.shape
    s_tot = n_pages * PAGE + PAGE
    kern = functools.partial(_idx_sample_kernel, layer=layer)
    pre = [("seq", iq), ("seq", iw), ("seq", new_idx)]
    keys = _paged_call(kern, page_table, pre, cache_idx, s_tot, I32, "idx_sample")
    rows = nb * SAMPLE_TP
    rb = _pick(rows, (32, 16, 8))
    cw = LANES * _largest_divisor(s_tot // LANES, 8)
    n_sel = min(TOPK_MAX, (n_pages * PAGE + t_real) // 4)
    thr, jl = pl.pallas_call(
        functools.partial(_thr_sample_kernel, nchunks=s_tot // cw, cw=cw, n_sel=n_sel,
                          idx_bits=int(s_tot).bit_length(), t_real=t_real),
        grid=(rows // rb,),
        in_specs=[pl.BlockSpec((rb, s_tot), lambda i: (i, 0))],
        out_specs=[pl.BlockSpec((rb, LANES), lambda i: (i, 0))] * 2,
        out_shape=[jax.ShapeDtypeStruct((rows, LANES), I32)] * 2,
        compiler_params=_cparams("parallel"), name="thr_sample",
    )(keys.reshape(rows, s_tot))
    kern = functools.partial(_dsa_sample_kernel, layer=layer)
    pre = [("seq", q), ("seq", keys), ("seq", thr.reshape(nb, SAMPLE_TP, LANES)),
           ("seq", jl.reshape(nb, SAMPLE_TP, LANES)), ("seq", new_kv)]
    return _paged_call(kern, page_table, pre, cache_kv, BRANCH_WIDTH, F32, "dsa_sample")


def _rms_matmul_kernel(x_ref, g_ref, w_ref, o_ref):
    o_ref[...] = _dot(_rms(x_ref[...], g_ref[...]).astype(BF16), w_ref[...])


def rms_matmul(x, g, w):
    n, d = x.shape
    m = w.shape[1]
    tm = _pick(n, (512, 256, 128, 64, 32, 16, 8))
    tn = _pick(m, (640, 512, 384, 256, 128))
    return pl.pallas_call(
        _rms_matmul_kernel, grid=(n // tm, m // tn),
        in_specs=[pl.BlockSpec((tm, d), lambda i, j: (i, 0)), pl.BlockSpec((1, d), lambda i, j: (0, 0)),
                  pl.BlockSpec((d, tn), lambda i, j: (0, j))],
        out_specs=pl.BlockSpec((tm, tn), lambda i, j: (i, j)),
        out_shape=jax.ShapeDtypeStruct((n, m), F32),
        compiler_params=_cparams("parallel", "arbitrary"), name="rms_matmul",
    )(x, g.reshape(1, d), w)


def _matmul_kernel(x_ref, w_ref, o_ref):
    o_ref[...] = _dot(x_ref[...].astype(BF16), w_ref[...])


def matmul(x, w):
    n, d = x.shape
    m = w.shape[1]
    tm = _pick(n, (512, 256, 128, 64, 32, 16, 8))
    return pl.pallas_call(
        _matmul_kernel, grid=(n // tm,),
        in_specs=[pl.BlockSpec((tm, d), lambda i: (i, 0)), pl.BlockSpec((d, m), lambda i: (0, 0))],
        out_specs=pl.BlockSpec((tm, m), lambda i: (i, 0)),
        out_shape=jax.ShapeDtypeStruct((n, m), F32),
        compiler_params=_cparams("parallel"), name="matmul",
    )(x, w)


def _merge_kernel(h_ref, g_ref, outs_ref, wgate_ref, wbr_ref, wout_ref, o_ref):
    h = h_ref[...]
    d = h.shape[1]
    a = _rms(h, g_ref[...]).astype(BF16)
    outs = outs_ref[...].astype(BF16)
    merged = None
    for n in range(N_BRANCH):
        gate = jax.nn.sigmoid(_dot(a, wgate_ref[:, n * d:(n + 1) * d]))
        y = _dot(outs[:, n * BRANCH_WIDTH:(n + 1) * BRANCH_WIDTH], wbr_ref[n])
        merged = gate * y if merged is None else merged + gate * y
    o_ref[...] = h + _dot(merged.astype(BF16), wout_ref[...])


def merge(h, g, outs, w_gate, w_branch, w_out):
    n, d = h.shape
    tm = _pick(n, (256, 128, 64, 32, 16, 8))
    full = lambda a: pl.BlockSpec(a.shape, lambda i, k=a.ndim: (0,) * k)
    return pl.pallas_call(
        _merge_kernel, grid=(n // tm,),
        in_specs=[pl.BlockSpec((tm, d), lambda i: (i, 0)), pl.BlockSpec((1, d), lambda i: (0, 0)),
                  pl.BlockSpec((tm, outs.shape[1]), lambda i: (i, 0)), full(w_gate), full(w_branch), full(w_out)],
        out_specs=pl.BlockSpec((tm, d), lambda i: (i, 0)),
        out_shape=jax.ShapeDtypeStruct((n, d), F32),
        compiler_params=_cparams("parallel"), name="merge",
    )(h, g.reshape(1, d), outs, w_gate, w_branch, w_out)


def _ffn_kernel(h_ref, g_ref, wg_ref, wu_ref, wd_ref, o_ref, c_scr, acc_scr):
    f = pl.program_id(1)

    @pl.when(f == 0)
    def _():
        c_scr[...] = _rms(h_ref[...], g_ref[...]).astype(BF16)
        acc_scr[...] = jnp.zeros_like(acc_scr)

    c = c_scr[...]
    hid = jax.nn.silu(_dot(c, wg_ref[...])) * _dot(c, wu_ref[...])
    acc_scr[...] += _dot(hid.astype(BF16), wd_ref[...])

    @pl.when(f == pl.num_programs(1) - 1)
    def _():
        o_ref[...] = h_ref[...] + acc_scr[...]


def ffn(h, g, wg, wu, wd):
    n, d = h.shape
    ff = wg.shape[1]
    tm = _pick(n, (1024, 512, 256, 128, 64, 32, 16, 8))
    tf = LANES * _largest_divisor(ff // LANES, 4)
    return pl.pallas_call(
        _ffn_kernel, grid=(n // tm, ff // tf),
        in_specs=[pl.BlockSpec((tm, d), lambda i, f: (i, 0)), pl.BlockSpec((1, d), lambda i, f: (0, 0)),
                  pl.BlockSpec((d, tf), lambda i, f: (0, f)), pl.BlockSpec((d, tf), lambda i, f: (0, f)),
                  pl.BlockSpec((tf, d), lambda i, f: (f, 0))],
        out_specs=pl.BlockSpec((tm, d), lambda i, f: (i, 0)),
        out_shape=jax.ShapeDtypeStruct((n, d), F32),
        scratch_shapes=[pltpu.VMEM((tm, d), BF16), pltpu.VMEM((tm, d), F32)],
        compiler_params=_cparams("parallel", "arbitrary"), name="ffn",
    )(h, g.reshape(1, d), wg, wu, wd)


def _split_bf16(x):
    hi = x.astype(BF16)
    return hi, (x - hi.astype(F32)).astype(BF16)


def _moe_kernel(h_ref, g_ref, wr_hi_ref, wr_lo_ref, wg_ref, wu_ref, wd_ref, o_ref, c_scr, comb_scr, acc_scr):
    e = pl.program_id(1)
    f = pl.program_id(2)
    lane = lax.broadcasted_iota(I32, (1, LANES), 1)

    @pl.when((e == 0) & (f == 0))
    def _():
        c = _rms(h_ref[...], g_ref[...])
        c_scr[...] = c.astype(BF16)
        c_hi, c_lo = _split_bf16(c)
        logits = _dot(c_hi, wr_hi_ref[...]) + (_dot(c_hi, wr_lo_ref[...]) + _dot(c_lo, wr_hi_ref[...]))
        logits = jnp.where(lane < N_EXPERTS, logits, -jnp.inf)
        m1 = jnp.max(logits, axis=1, keepdims=True)
        i1 = jnp.min(jnp.where(logits == m1, lane, LANES), axis=1, keepdims=True)
        rest = jnp.where(lane == i1, -jnp.inf, logits)
        m2 = jnp.max(rest, axis=1, keepdims=True)
        i2 = jnp.min(jnp.where(rest == m2, lane, LANES), axis=1, keepdims=True)
        r = jnp.exp(m2 - m1)
        w1 = 1.0 / (1.0 + r)
        comb_scr[...] = jnp.where(lane == i1, w1, 0.0) + jnp.where(lane == i2, r * w1, 0.0)
        acc_scr[...] = jnp.zeros_like(acc_scr)

    c = c_scr[...]
    ce = jnp.sum(jnp.where(lane == e, comb_scr[...], 0.0), axis=1, keepdims=True)
    hid = jax.nn.silu(_dot(c, wg_ref[...])) * _dot(c, wu_ref[...]) * ce
    acc_scr[...] += _dot(hid.astype(BF16), wd_ref[...])

    @pl.when((e == pl.num_programs(1) - 1) & (f == pl.num_programs(2) - 1))
    def _():
        o_ref[...] = h_ref[...] + acc_scr[...]


def moe(h, g, w_router, wg, wu, wd):
    n, d = h.shape
    ne, _, ff = wg.shape
    tm = _pick(n, (1024, 512, 256, 128, 64, 32, 16, 8))
    tf = LANES * _largest_divisor(ff // LANES, 4)
    wr = jnp.pad(w_router, ((0, 0), (0, LANES - ne)))
    wr_hi = wr.astype(BF16)
    wr_lo = (wr - wr_hi.astype(F32)).astype(BF16)
    return pl.pallas_call(
        _moe_kernel, grid=(n // tm, ne, ff // tf),
        in_specs=[pl.BlockSpec((tm, d), lambda i, e, f: (i, 0)), pl.BlockSpec((1, d), lambda i, e, f: (0, 0)),
                  pl.BlockSpec((d, LANES), lambda i, e, f: (0, 0)), pl.BlockSpec((d, LANES), lambda i, e, f: (0, 0)),
                  pl.BlockSpec((None, d, tf), lambda i, e, f: (e, 0, f)),
                  pl.BlockSpec((None, d, tf), lambda i, e, f: (e, 0, f)),
                  pl.BlockSpec((None, tf, d), lambda i, e, f: (e, f, 0))],
        out_specs=pl.BlockSpec((tm, d), lambda i, e, f: (i, 0)),
        out_shape=jax.ShapeDtypeStruct((n, d), F32),
        scratch_shapes=[pltpu.VMEM((tm, d), BF16), pltpu.VMEM((tm, LANES), F32), pltpu.VMEM((tm, d), F32)],
        compiler_params=_cparams("parallel", "arbitrary", "arbitrary"), name="moe",
    )(h, g.reshape(1, d), wr_hi, wr_lo, wg, wu, wd)


def _ple_kernel(h_ref, g_ref, p_ref, wgate_ref, wple_ref, o_ref):
    h = h_ref[...]
    gate = jax.nn.sigmoid(_dot(_rms(h, g_ref[...]).astype(BF16), wgate_ref[...]))
    o_ref[...] = h + gate * _dot(p_ref[...].astype(BF16), wple_ref[...])


def ple(h, g, p, w_gate, w_ple):
    n, d = h.shape
    tm = _pick(n, (512, 256, 128, 64, 32, 16, 8))
    full = lambda a: pl.BlockSpec(a.shape, lambda i, k=a.ndim: (0,) * k)
    return pl.pallas_call(
        _ple_kernel, grid=(n // tm,),
        in_specs=[pl.BlockSpec((tm, d), lambda i: (i, 0)), pl.BlockSpec((1, d), lambda i: (0, 0)),
                  pl.BlockSpec((tm, p.shape[1]), lambda i: (i, 0)), full(w_gate), full(w_ple)],
        out_specs=pl.BlockSpec((tm, d), lambda i: (i, 0)),
        out_shape=jax.ShapeDtypeStruct((n, d), F32),
        compiler_params=_cparams("parallel"), name="ple",
    )(h, g.reshape(1, d), p, w_gate, w_ple)


def _rmsnorm(x, g):
    return x * lax.rsqrt(jnp.mean(x * x, axis=-1, keepdims=True) + RMS_EPS) * g


def _rope(x, pos):
    half = x.shape[-1] // 2
    inv_freq = ROPE_THETA ** (-jnp.arange(half, dtype=F32) / half)
    ang = pos.astype(F32)[:, None] * inv_freq[None, :]
    ang = ang.reshape((ang.shape[0],) + (1,) * (x.ndim - 2) + (half,))
    cos, sin = jnp.cos(ang), jnp.sin(ang)
    x1, x2 = x[..., :half], x[..., half:]
    return jnp.concatenate([x1 * cos - x2 * sin, x1 * sin + x2 * cos], axis=-1)


def _block_diag_uk(w_uk):
    c, h, e = w_uk.shape
    out = jnp.zeros((h * e, h * c), w_uk.dtype)
    for i in range(h):
        out = out.at[i * e:(i + 1) * e, i * c:(i + 1) * c].set(w_uk[:, i, :].T)
    return out


def _token_features(z, pos, w, i):
    n = z.shape[0]
    h = N_HEADS
    split_at = [int(v) for v in np.cumsum(IN_SPLITS)[:-1]]
    (sb_q, sb_k, sb_v, df_q, df_k, df_v, m_cq, m_ckv, m_kpe,
     ds_q, ds_k, ds_v, ix_q, ix_k, ix_w) = jnp.split(z[:, :IN_COLS], split_at, axis=-1)
    f = {}
    f["sb_q"] = sb_q
    f["sb_row"] = jnp.concatenate([sb_k, sb_v], axis=-1)
    df_q = _rope(_rmsnorm(df_q.reshape(n, h, 2, DIFF_QK), w["diff_q_norm"][i]), pos)
    df_k = _rope(_rmsnorm(df_k.reshape(n, 2, DIFF_QK), w["diff_k_norm"][i]), pos)
    f["df_q"] = df_q.reshape(n, h * 2 * DIFF_QK)
    f["diff_row"] = jnp.concatenate([df_k.reshape(n, 2 * DIFF_QK), df_v], axis=-1)
    w_uq = w["w_mla_uq"][i].reshape(MLA_Q_RANK, h * (MLA_NOPE + MLA_ROPE)).astype(BF16)
    q_full = matmul(_rmsnorm(m_cq, w["mla_q_norm_a"][i]), w_uq).reshape(n, h, MLA_NOPE + MLA_ROPE)
    q_nope = _rmsnorm(q_full[..., :MLA_NOPE], w["mla_qn_nope"][i])
    q_pe = _rope(_rmsnorm(q_full[..., MLA_NOPE:], w["mla_qn_pe"][i]), pos)
    f["q_lat"] = matmul(q_nope.reshape(n, h * MLA_NOPE), _block_diag_uk(w["w_mla_uk"][i]).astype(BF16))
    f["q_pe"] = q_pe.reshape(n, h * MLA_ROPE)
    f["mla_row"] = jnp.concatenate([_rmsnorm(m_ckv, w["mla_kv_norm"][i]),
                                    _rope(_rmsnorm(m_kpe, w["mla_kn_pe"][i]), pos)], axis=-1)
    f["ds_q"] = _rope(_rmsnorm(ds_q.reshape(n, h, HEAD_DIM), w["dsa_q_norm"][i]), pos).reshape(n, h * HEAD_DIM)
    f["ds_row"] = jnp.concatenate([_rope(_rmsnorm(ds_k, w["dsa_k_norm"][i]), pos), ds_v], axis=-1)
    f["ix_q"] = _rope(ix_q.reshape(n, IDX_HEADS, IDX_DIM), pos).reshape(n, IDX_HEADS * IDX_DIM)
    f["idx_row"] = _rope(ix_k, pos)
    f["ix_w"] = ix_w
    return f


def kernel(x_prompt, x_sample, cache_sb, cache_diff, cache_mla, cache_dsa_kv, cache_dsa_idx, page_table, p_prompt, p_sample, norm_mix, w_in, diff_q_norm, diff_k_norm, diff_lq1, diff_lk1, diff_lq2, diff_lk2, diff_subln, mla_q_norm_a, mla_kv_norm, w_mla_uq, w_mla_uk, w_mla_uv, mla_qn_nope, mla_qn_pe, mla_kn_pe, dsa_q_norm, dsa_k_norm, w_branch, w_gate, w_out, norm_ffn, w_ff_gate, w_ff_up, w_ff_down, w_router, w_moe_gate, w_moe_up, w_moe_down, norm_ple, w_ple_gate, w_ple):
    w = dict(diff_q_norm=diff_q_norm, diff_k_norm=diff_k_norm, mla_q_norm_a=mla_q_norm_a,
             mla_kv_norm=mla_kv_norm, w_mla_uq=w_mla_uq, w_mla_uk=w_mla_uk, mla_qn_nope=mla_qn_nope,
             mla_qn_pe=mla_qn_pe, mla_kn_pe=mla_kn_pe, dsa_q_norm=dsa_q_norm, dsa_k_norm=dsa_k_norm)
    bp, tp, d = x_prompt.shape
    bs, ts, _ = x_sample.shape
    depth = w_in.shape[0]
    n_pages = page_table.shape[1]
    past_len = n_pages * PAGE
    npr, nsm = bp * tp, bs * ts
    n_tok = npr + nsm
    n_pad = -n_tok % 256
    pos = jnp.concatenate([jnp.tile(jnp.arange(tp, dtype=I32), bp),
                           jnp.tile(past_len + jnp.arange(ts, dtype=I32), bs),
                           jnp.zeros((n_pad,), I32)])
    h = jnp.concatenate([x_prompt.reshape(npr, d), x_sample.reshape(nsm, d), jnp.zeros((n_pad, d), F32)])
    in_pad = -IN_COLS % LANES
    names = ("sb_row", "diff_row", "mla_row", "ds_row", "idx_row")
    new_rows = {k: [] for k in names}

    for i in range(depth):
        z = rms_matmul(h, norm_mix[i], jnp.pad(w_in[i], ((0, 0), (0, in_pad))).astype(BF16))
        f = _token_features(z, pos, w, i)
        pr = lambda a: a[:npr].reshape(bp, tp, a.shape[-1])
        sm = lambda a: a[npr:n_tok].reshape(bs, ts, a.shape[-1])
        smq = lambda a: _pad_rows(sm(a), SAMPLE_TP)
        smk = lambda a: _pad_rows(sm(a), PAGE)
        for k in names:
            new_rows[k].append((pr(f[k]), sm(f[k])))

        lam_init = 0.8 - 0.6 * math.exp(-0.3 * i)
        lam = (jnp.exp(jnp.sum(diff_lq1[i] * diff_lk1[i])) - jnp.exp(jnp.sum(diff_lq2[i] * diff_lk2[i]))
               + lam_init).reshape(1).astype(F32)
        subln = diff_subln[i].reshape(1, HEAD_DIM)
        wuv = jnp.transpose(w_mla_uv[i], (1, 0, 2)).astype(BF16)

        outs_p = [sb_prompt(pr(f["sb_q"]), pr(f["sb_row"])),
                  diff_prompt(lam, pr(f["df_q"]), pr(f["diff_row"]), subln, 1.0 - lam_init),
                  mla_prompt(pr(f["q_lat"]), pr(f["q_pe"]), pr(f["mla_row"]), wuv),
                  dsa_prompt(pr(f["ix_q"]), pr(f["ix_w"]), pr(f["idx_row"]), pr(f["ds_q"]), pr(f["ds_row"]))]
        outs_s = [sb_sample(page_table, smq(f["sb_q"]), smk(f["sb_row"]), cache_sb, i),
                  diff_sample(page_table, lam, smq(f["df_q"]), smk(f["diff_row"]), subln, cache_diff, i,
                              1.0 - lam_init),
                  mla_sample(page_table, smq(f["q_lat"]), smq(f["q_pe"]), smk(f["mla_row"]), wuv, cache_mla, i),
                  dsa_sample(page_table, smq(f["ix_q"]), smq(f["ix_w"]), smk(f["idx_row"]), cache_dsa_idx,
                             smq(f["ds_q"]), smk(f["ds_row"]), cache_dsa_kv, i, ts)]
        outs = jnp.concatenate([
            jnp.concatenate([o.reshape(npr, BRANCH_WIDTH) for o in outs_p], axis=1),
            jnp.concatenate([o[:, :ts].reshape(nsm, BRANCH_WIDTH) for o in outs_s], axis=1),
            jnp.zeros((n_pad, N_BRANCH * BRANCH_WIDTH), F32)])
        h = merge(h, norm_mix[i], outs, w_gate[i].astype(BF16), w_branch[i].astype(BF16), w_out[i].astype(BF16))
        j = i // 2
        if i % 2 == 0:
            h = ffn(h, norm_ffn[i], w_ff_gate[j].astype(BF16), w_ff_up[j].astype(BF16), w_ff_down[j].astype(BF16))
        else:
            h = moe(h, norm_ffn[i], w_router[j], w_moe_gate[j].astype(BF16), w_moe_up[j].astype(BF16),
                    w_moe_down[j].astype(BF16))
        p = jnp.concatenate([p_prompt[i].reshape(npr, -1), p_sample[i].reshape(nsm, -1),
                             jnp.zeros((n_pad, p_prompt.shape[-1]), F32)])
        h = ple(h, norm_ple[i], p, w_ple_gate[i].astype(BF16), w_ple[i].astype(BF16))

    y_prompt = h[:npr].reshape(bp, tp, d)
    y_sample = h[npr:n_tok].reshape(bs, ts, d)
    state = []
    for k in names:
        state.append(jnp.stack([r[0] for r in new_rows[k]], axis=0))
        state.append(jnp.stack([r[1] for r in new_rows[k]], axis=0))
    return (y_prompt, y_sample) + tuple(state)
```

```python
import functools
import math

import numpy as np
import jax
import jax.numpy as jnp
from jax import lax
from jax.experimental import pallas as pl
from jax.experimental.pallas import tpu as pltpu

F32, BF16, I32 = jnp.float32, jnp.bfloat16, jnp.int32

N_HEADS = 4
HEAD_DIM = 64
N_BRANCH = 4
BRANCH_WIDTH = N_HEADS * HEAD_DIM
DIFF_QK = 32
MLA_Q_RANK = 256
MLA_KV_RANK = 128
MLA_NOPE = 64
MLA_ROPE = 32
IDX_HEADS = 8
IDX_DIM = 32
TOPK_MAX = 256
N_EXPERTS = 8
ROPE_THETA = 10000.0
RMS_EPS = 1e-6
PAGE = 128
IN_SPLITS = (
    N_HEADS * HEAD_DIM, HEAD_DIM, HEAD_DIM,
    N_HEADS * 2 * DIFF_QK, 2 * DIFF_QK, HEAD_DIM,
    MLA_Q_RANK, MLA_KV_RANK, MLA_ROPE,
    N_HEADS * HEAD_DIM, HEAD_DIM, HEAD_DIM,
    IDX_HEADS * IDX_DIM, IDX_DIM, IDX_HEADS,
)
IN_COLS = sum(IN_SPLITS)

LANES = 128
SUBLANES = 8
VMEM_LIMIT = 52 * 1024 * 1024
NEG = -1e30
INT_MIN = -2 ** 31
IDX_BIG = 2 ** 30
SAMPLE_TP = SUBLANES
PAGES_PER_GROUP = 32


def _cparams(*sem):
    return pltpu.CompilerParams(dimension_semantics=sem, vmem_limit_bytes=VMEM_LIMIT)


def _dot(a, b):
    return jnp.dot(a, b, preferred_element_type=F32)


def _dot_nt(a, b):
    return lax.dot_general(a, b, (((1,), (1,)), ((), ())), preferred_element_type=F32)


def _pick(n, cands):
    for c in cands:
        if n % c == 0:
            return c
    return n


def _largest_divisor(n, maxd):
    for d in range(min(n, maxd), 0, -1):
        if n % d == 0:
            return d
    return 1


def _rms(x, g):
    return x * lax.rsqrt(jnp.mean(x * x, axis=-1, keepdims=True) + RMS_EPS) * g


def _heads_to_rows(x, width, starts):
    return jnp.concatenate([x[:, s:s + width] for s in starts], axis=0)


def _rows_to_heads(x, n, t):
    return jnp.concatenate([x[h * t:(h + 1) * t] for h in range(n)], axis=1)


def _tri(n):
    return (lax.broadcasted_iota(I32, (n, n), 0) >= lax.broadcasted_iota(I32, (n, n), 1)).astype(BF16)


def _sb_block(q, kv, carry, acc, tri, vis):
    n, ck = kv.shape[0], tri.shape[0]
    k = kv[:, :HEAD_DIM].astype(BF16)
    v = kv[:, HEAD_DIM:].astype(BF16)
    z = _dot_nt(q, k)
    t = jnp.log1p(jnp.exp(-jnp.abs(z)))
    log_keep = -(jnp.maximum(z, 0.0) + t)
    log_beta = jnp.minimum(z, 0.0) - t
    if vis is not None:
        log_keep = jnp.where(vis, log_keep, 0.0)
    incl = []
    for c in range(n // ck):
        lk = log_keep[:, c * ck:(c + 1) * ck]
        hi = lk.astype(BF16)
        lo = (lk - hi.astype(F32)).astype(BF16)
        incl.append(_dot(hi, tri) + _dot(lo, tri))
    later = [None] * len(incl)
    for c in reversed(range(len(incl))):
        later[c] = incl[c] + carry
        carry = carry + incl[c][:, :1]
    w = jnp.exp(log_beta + (jnp.concatenate(later, axis=1) - log_keep))
    if vis is not None:
        w = jnp.where(vis, w, 0.0)
    return carry, acc + _dot(w.astype(BF16), v)


def _sm_block(s, v, m, l, acc, v_keys_minor=False):
    m_new = jnp.maximum(m, jnp.max(s, axis=1, keepdims=True))
    alpha = jnp.exp(m - m_new)
    p = jnp.exp(s - m_new)
    l = alpha * l + jnp.sum(p, axis=1, keepdims=True)
    pv = _dot_nt(p.astype(BF16), v) if v_keys_minor else _dot(p.astype(BF16), v)
    return m_new, l, alpha * acc + pv


def _sm_init(rows, width):
    return (jnp.full((rows, 1), NEG, F32), jnp.zeros((rows, 1), F32), jnp.zeros((rows, width), F32))


def _diff_queries(x):
    return [_heads_to_rows(x, DIFF_QK, [h * 2 * DIFF_QK + c * DIFF_QK for h in range(N_HEADS)]).astype(BF16)
            for c in range(2)]


def _diff_step(qs, kv, st, vis):
    v = kv[:, 2 * DIFF_QK:].astype(BF16)
    out = []
    for c in range(2):
        k = kv[:, c * DIFF_QK:(c + 1) * DIFF_QK].astype(BF16)
        s = _dot_nt(qs[c], k) * (DIFF_QK ** -0.5)
        if vis is not None:
            s = jnp.where(vis, s, NEG)
        out.extend(_sm_block(s, v, *st[3 * c:3 * c + 3]))
    return tuple(out)


def _diff_finish(st, lam, g, post_scale, t):
    o = st[2] * (1.0 / st[1]) - lam * (st[5] * (1.0 / st[4]))
    o = _rms(o, g) * post_scale
    return _rows_to_heads(o, N_HEADS, t)


def _mla_queries(xl, xp):
    ql = _heads_to_rows(xl, MLA_KV_RANK, [h * MLA_KV_RANK for h in range(N_HEADS)]).astype(BF16)
    qp = _heads_to_rows(xp, MLA_ROPE, [h * MLA_ROPE for h in range(N_HEADS)]).astype(BF16)
    return ql, qp


def _mla_step(ql, qp, kv, st, vis, keys_minor=False):
    if keys_minor:
        c = kv[:MLA_KV_RANK].astype(BF16)
        s = _dot(ql, c) + _dot(qp, kv[MLA_KV_RANK:].astype(BF16))
    else:
        c = kv[:, :MLA_KV_RANK].astype(BF16)
        s = _dot_nt(ql, c) + _dot_nt(qp, kv[:, MLA_KV_RANK:].astype(BF16))
    s = s * ((MLA_NOPE + MLA_ROPE) ** -0.5)
    if vis is not None:
        s = jnp.where(vis, s, NEG)
    return _sm_block(s, c, *st, v_keys_minor=keys_minor)


def _mla_finish(st, wuv_ref, t):
    lat = (st[2] * (1.0 / st[1])).astype(BF16)
    return jnp.concatenate([_dot(lat[h * t:(h + 1) * t], wuv_ref[h]) for h in range(N_HEADS)], axis=1)


def _idx_queries(x):
    return _heads_to_rows(x, IDX_DIM, [h * IDX_DIM for h in range(IDX_HEADS)]).astype(BF16)


def _idx_weights(iw, t):
    return [jnp.broadcast_to(iw[:, h:h + 1], (t, LANES)) for h in range(IDX_HEADS)]


def _idx_score(iq, iwb, kxt, t):
    r = _dot(iq, kxt.astype(BF16))
    tiles = []
    for j in range(kxt.shape[1] // LANES):
        sl = slice(j * LANES, (j + 1) * LANES)
        sc = jnp.maximum(r[:t, sl], 0.0) * iwb[0]
        for h in range(1, IDX_HEADS):
            sc = sc + jnp.maximum(r[h * t:(h + 1) * t, sl], 0.0) * iwb[h]
        tiles.append(sc)
    return jnp.concatenate(tiles, axis=1)


def _sortable(x):
    x = jnp.where(x == 0.0, 0.0, x)
    b = pltpu.bitcast(x, I32)
    return b ^ ((b >> 31) & jnp.int32(0x7FFFFFFF))


def _count(ref, nchunks, cw, pred):
    rows = ref.shape[0]

    def body(c, acc):
        base = pl.multiple_of(c * cw, LANES)
        m = pred(ref[:, pl.ds(base, cw)], base).astype(I32)
        for j in range(cw // LANES):
            acc = acc + m[:, j * LANES:(j + 1) * LANES]
        return acc

    acc = lax.fori_loop(0, nchunks, body, jnp.zeros((rows, LANES), I32))
    return jnp.sum(acc, axis=1, keepdims=True)


def _select_threshold(ref, nchunks, cw, k, idx_bits, real_rows):
    rows = ref.shape[0]

    def bit_body(i, u):
        cand_u = u | lax.shift_left(jnp.int32(1), 31 - i)
        cand = cand_u ^ jnp.int32(INT_MIN)
        cnt = _count(ref, nchunks, cw, lambda blk, base: blk >= cand)
        return jnp.where(cnt >= k, cand_u, u)

    u = lax.fori_loop(0, 32, bit_body, jnp.zeros((rows, 1), I32))
    thr = u ^ jnp.int32(INT_MIN)
    cnt_gt = _count(ref, nchunks, cw, lambda blk, base: blk > thr)
    cnt_ge = _count(ref, nchunks, cw, lambda blk, base: blk >= thr)
    need = k - cnt_gt
    tie = (cnt_ge > k) & (thr > INT_MIN)
    if real_rows is not None:
        tie = tie & real_rows

    def tie_path():
        def jbody(i, j):
            cand = j | lax.shift_left(jnp.int32(1), idx_bits - 1 - i)

            def pred(blk, base):
                kid = base + lax.broadcasted_iota(I32, (1, cw), 1)
                return (blk == thr) & (kid < cand)

            cnt = _count(ref, nchunks, cw, pred)
            return jnp.where(cnt <= need, cand, j)

        return lax.fori_loop(0, idx_bits, jbody, jnp.zeros((rows, 1), I32))

    jl = lax.cond(jnp.max(tie.astype(I32)) > 0, tie_path, lambda: jnp.zeros((rows, 1), I32))
    jl = jnp.where(tie, jl, jnp.where(thr > INT_MIN, IDX_BIG, 0))
    return thr, jl


def _dsa_bias(key, thr, jl, base, heads):
    n = key.shape[1]
    kid = base + lax.broadcasted_iota(I32, (1, n), 1)
    sel = (key > thr) | ((key == thr) & (kid < jl))
    bias = jnp.where(sel, 0.0, NEG)
    return jnp.concatenate([bias] * heads, axis=0)


def _kblock(ref, kb, tq):
    return ref[0, pl.ds(pl.multiple_of(kb * tq, tq), tq), :]


def _causal_vis(qi, tq, kb, tk, heads, strict):
    qpos = qi * tq + (lax.broadcasted_iota(I32, (heads * tq, 1), 0) & (tq - 1))
    kpos = kb * tk + lax.broadcasted_iota(I32, (1, tk), 1)
    return (kpos < qpos) if strict else (kpos <= qpos)


def _sb_prompt_kernel(q_ref, kv_ref, o_ref, *, tq, tk):
    qi = pl.program_id(1)
    nfull = (qi * tq) // tk
    rows = N_HEADS * tq
    q = (_heads_to_rows(q_ref[0], HEAD_DIM, [h * HEAD_DIM for h in range(N_HEADS)])
         * (HEAD_DIM ** -0.5)).astype(BF16)
    tri = _tri(min(tk, 2 * LANES))
    st = (jnp.zeros((rows, 1), F32), jnp.zeros((rows, HEAD_DIM), F32))
    st = _sb_block(q, _kblock(kv_ref, nfull, tk), *st, tri, _causal_vis(qi, tq, nfull, tk, N_HEADS, True))
    st = lax.fori_loop(0, nfull, lambda i, c: _sb_block(q, _kblock(kv_ref, nfull - 1 - i, tk), *c, tri, None), st)
    o_ref[0] = _rows_to_heads(st[1], N_HEADS, tq)


def _diff_prompt_kernel(lam_ref, q_ref, kv_ref, g_ref, o_ref, *, tq, tk, post_scale):
    qi = pl.program_id(1)
    nfull = (qi * tq) // tk
    rows = N_HEADS * tq
    qs = _diff_queries(q_ref[0])
    st = _sm_init(rows, HEAD_DIM) * 2
    st = lax.fori_loop(0, nfull, lambda i, c: _diff_step(qs, _kblock(kv_ref, i, tk), c, None), st)
    st = _diff_step(qs, _kblock(kv_ref, nfull, tk), st, _causal_vis(qi, tq, nfull, tk, N_HEADS, False))
    o_ref[0] = _diff_finish(st, lam_ref[0], g_ref[...], post_scale, tq)


def _mla_prompt_kernel(ql_ref, qp_ref, kv_ref, wuv_ref, o_ref, *, tq, tk):
    qi = pl.program_id(1)
    nfull = (qi * tq) // tk
    rows = N_HEADS * tq
    ql, qp = _mla_queries(ql_ref[0], qp_ref[0])
    st = _sm_init(rows, MLA_KV_RANK)
    st = lax.fori_loop(0, nfull, lambda i, c: _mla_step(ql, qp, _kblock(kv_ref, i, tk), c, None), st)
    st = _mla_step(ql, qp, _kblock(kv_ref, nfull, tk), st, _causal_vis(qi, tq, nfull, tk, N_HEADS, False))
    o_ref[0] = _mla_finish(st, wuv_ref, tq)


def _dsa_prompt_kernel(iq_ref, iw_ref, kidxt_ref, q_ref, kv_ref, o_ref, key_scr, *, tq, tk, n_sel, idx_bits):
    qi = pl.program_id(1)
    nfull = (qi * tq) // tk
    iq = _idx_queries(iq_ref[0])
    iwb = _idx_weights(iw_ref[0], tq)

    def score(kb, vis):
        base = pl.multiple_of(kb * tk, tk)
        key = _sortable(_idx_score(iq, iwb, kidxt_ref[0, :, pl.ds(base, tk)], tq))
        if vis is not None:
            key = jnp.where(vis, key, INT_MIN)
        key_scr[:, pl.ds(base, tk)] = key

    def score_body(i, c):
        score(i, None)
        return c

    lax.fori_loop(0, nfull, score_body, 0)
    score(nfull, _causal_vis(qi, tq, nfull, tk, 1, False))
    thr, jl = _select_threshold(key_scr, nfull + 1, tk, n_sel, idx_bits, None)

    q = (_heads_to_rows(q_ref[0], HEAD_DIM, [h * HEAD_DIM for h in range(N_HEADS)])
         * (HEAD_DIM ** -0.5)).astype(BF16)

    def att(kb, st):
        kv = _kblock(kv_ref, kb, tk)
        base = pl.multiple_of(kb * tk, tk)
        bias = _dsa_bias(key_scr[:, pl.ds(base, tk)], thr, jl, base, N_HEADS)
        s = _dot_nt(q, kv[:, :HEAD_DIM].astype(BF16)) + bias
        return _sm_block(s, kv[:, HEAD_DIM:].astype(BF16), *st)

    st = lax.fori_loop(0, nfull + 1, att, _sm_init(N_HEADS * tq, HEAD_DIM))
    o_ref[0] = _rows_to_heads(st[2] * (1.0 / st[1]), N_HEADS, tq)


def _prompt_specs(tq, widths, t_total, kv_widths):
    qs = [pl.BlockSpec((1, tq, w), lambda b, i: (b, i, 0)) for w in widths]
    ks = [pl.BlockSpec((1, t_total, w), lambda b, i: (b, 0, 0)) for w in kv_widths]
    return qs, ks


def _prompt_tiles(t, tq_want):
    tq = _pick(t, tuple(c for c in (256, 128, 64, 32, 16, 8) if c <= tq_want))
    tk = _pick(t, tuple(c for c in (512, 256, 128, 64, 32, 16, 8) if c >= tq))
    return tq, tk


def sb_prompt(q, kv):
    b, t, _ = q.shape
    tq, tk = _prompt_tiles(t, 256)
    qs, ks = _prompt_specs(tq, [BRANCH_WIDTH], t, [kv.shape[-1]])
    return pl.pallas_call(
        functools.partial(_sb_prompt_kernel, tq=tq, tk=tk),
        grid=(b, t // tq), in_specs=qs + ks,
        out_specs=pl.BlockSpec((1, tq, BRANCH_WIDTH), lambda b, i: (b, i, 0)),
        out_shape=jax.ShapeDtypeStruct((b, t, BRANCH_WIDTH), F32),
        compiler_params=_cparams("parallel", "arbitrary"), name="sb_prompt",
    )(q, kv)


def diff_prompt(lam, q, kv, g, post_scale):
    b, t, _ = q.shape
    tq, tk = _prompt_tiles(t, 256)
    qs, ks = _prompt_specs(tq, [BRANCH_WIDTH], t, [kv.shape[-1]])
    return pl.pallas_call(
        functools.partial(_diff_prompt_kernel, tq=tq, tk=tk, post_scale=post_scale),
        grid=(b, t // tq),
        in_specs=[pl.BlockSpec(memory_space=pltpu.SMEM)] + qs + ks
        + [pl.BlockSpec((1, HEAD_DIM), lambda b, i: (0, 0))],
        out_specs=pl.BlockSpec((1, tq, BRANCH_WIDTH), lambda b, i: (b, i, 0)),
        out_shape=jax.ShapeDtypeStruct((b, t, BRANCH_WIDTH), F32),
        compiler_params=_cparams("parallel", "arbitrary"), name="diff_prompt",
    )(lam, q, kv, g)


def mla_prompt(ql, qp, kv, wuv):
    b, t, _ = ql.shape
    tq, tk = _prompt_tiles(t, 256)
    qs, ks = _prompt_specs(tq, [ql.shape[-1], qp.shape[-1]], t, [kv.shape[-1]])
    return pl.pallas_call(
        functools.partial(_mla_prompt_kernel, tq=tq, tk=tk),
        grid=(b, t // tq),
        in_specs=qs + ks + [pl.BlockSpec(wuv.shape, lambda b, i: (0, 0, 0))],
        out_specs=pl.BlockSpec((1, tq, BRANCH_WIDTH), lambda b, i: (b, i, 0)),
        out_shape=jax.ShapeDtypeStruct((b, t, BRANCH_WIDTH), F32),
        compiler_params=_cparams("parallel", "arbitrary"), name="mla_prompt",
    )(ql, qp, kv, wuv)


def dsa_prompt(iq, iw, kidxt, q, kv):
    b, t, _ = q.shape
    tq, tk = _prompt_tiles(t, 256)
    n_sel = min(TOPK_MAX, t // 4)
    qs, _ = _prompt_specs(tq, [iq.shape[-1], iw.shape[-1]], t, [])
    ks = [pl.BlockSpec((1, kidxt.shape[1], t), lambda b, i: (b, 0, 0))]
    qs2, ks2 = _prompt_specs(tq, [BRANCH_WIDTH], t, [kv.shape[-1]])
    return pl.pallas_call(
        functools.partial(_dsa_prompt_kernel, tq=tq, tk=tk, n_sel=n_sel, idx_bits=int(t).bit_length()),
        grid=(b, t // tq),
        in_specs=qs + ks + qs2 + ks2,
        out_specs=pl.BlockSpec((1, tq, BRANCH_WIDTH), lambda b, i: (b, i, 0)),
        out_shape=jax.ShapeDtypeStruct((b, t, BRANCH_WIDTH), F32),
        scratch_shapes=[pltpu.VMEM((tq, t), I32)],
        compiler_params=_cparams("parallel", "arbitrary"), name="dsa_prompt",
    )(iq, iw, kidxt, q, kv)


class _Pager:
    def __init__(self, pt_ref, cache_ref, buf, sem, layer, n_pages, group, keys_minor=False):
        self.pt, self.cache, self.buf, self.sem = pt_ref, cache_ref, buf, sem
        self.layer, self.n_pages, self.group = layer, n_pages, group
        self.n_groups = n_pages // group
        self.keys_minor = keys_minor

    def _copy(self, page, slot, p):
        if self.keys_minor:
            dst = self.buf.at[slot, :, pl.ds(p * PAGE, PAGE)]
        else:
            dst = self.buf.at[slot, pl.ds(p * PAGE, PAGE)]
        return pltpu.make_async_copy(self.cache.at[self.layer, page], dst, self.sem.at[slot])

    def first_page(self, g):
        return self.n_pages - (g + 1) * self.group

    def start(self, b, g, slot):
        base = self.first_page(g)
        for p in range(self.group):
            self._copy(self.pt[b, base + p], slot, p).start()

    def wait(self, slot):
        for p in range(self.group):
            self._copy(0, slot, p).wait()

    def sweep(self, body, init):
        b = pl.program_id(0)
        nb = pl.num_programs(0)
        ng = self.n_groups

        @pl.when(b == 0)
        def _():
            self.start(0, 0, 0)

        def gbody(g, carry):
            slot = (b * ng + g) % 2

            @pl.when(g + 1 < ng)
            def _():
                self.start(b, g + 1, 1 - slot)

            @pl.when((g + 1 == ng) & (b + 1 < nb))
            def _():
                self.start(b + 1, 0, 1 - slot)

            self.wait(slot)
            return body(g, self.buf.at[slot], carry)

        return lax.fori_loop(0, ng, gbody, init)


def _sample_vis(heads, strict):
    row = lax.broadcasted_iota(I32, (heads * SAMPLE_TP, 1), 0) & (SAMPLE_TP - 1)
    col = lax.broadcasted_iota(I32, (1, PAGE), 1)
    return (col < row) if strict else (col <= row)


def _sb_sample_kernel(pt_ref, q_ref, new_ref, cache_ref, o_ref, buf, sem, *, layer, n_pages, group):
    tp = SAMPLE_TP
    rows = N_HEADS * tp
    pager = _Pager(pt_ref, cache_ref, buf, sem, layer, n_pages, group)
    q = (_heads_to_rows(q_ref[0], HEAD_DIM, [h * HEAD_DIM for h in range(N_HEADS)])
         * (HEAD_DIM ** -0.5)).astype(BF16)
    tri = _tri(min(2 * PAGE, group * PAGE))
    st = (jnp.zeros((rows, 1), F32), jnp.zeros((rows, HEAD_DIM), F32))
    st = _sb_block(q, new_ref[0], *st, _tri(PAGE), _sample_vis(N_HEADS, True))
    st = pager.sweep(lambda g, rows_ref, st: _sb_block(q, rows_ref[...], *st, tri, None), st)
    o_ref[0] = _rows_to_heads(st[1], N_HEADS, tp)


def _diff_sample_kernel(pt_ref, lam_ref, q_ref, new_ref, g_ref, cache_ref, o_ref, buf, sem, *,
                        layer, n_pages, group, post_scale):
    tp = SAMPLE_TP
    pager = _Pager(pt_ref, cache_ref, buf, sem, layer, n_pages, group)
    qs = _diff_queries(q_ref[0])
    st = _sm_init(N_HEADS * tp, HEAD_DIM) * 2
    st = _diff_step(qs, new_ref[0], st, _sample_vis(N_HEADS, False))
    st = pager.sweep(lambda g, rows_ref, st: _diff_step(qs, rows_ref[...], st, None), st)
    o_ref[0] = _diff_finish(st, lam_ref[0], g_ref[...], post_scale, tp)


def _mla_sample_kernel(pt_ref, ql_ref, qp_ref, new_ref, wuv_ref, cache_ref, o_ref, buf, sem, *,
                       layer, n_pages, group):
    tp = SAMPLE_TP
    pager = _Pager(pt_ref, cache_ref, buf, sem, layer, n_pages, group, keys_minor=True)
    ql, qp = _mla_queries(ql_ref[0], qp_ref[0])
    st = _sm_init(N_HEADS * tp, MLA_KV_RANK)
    st = _mla_step(ql, qp, new_ref[0], st, _sample_vis(N_HEADS, False), keys_minor=True)
    st = pager.sweep(lambda g, rows_ref, st: _mla_step(ql, qp, rows_ref[...], st, None, keys_minor=True), st)
    o_ref[0] = _mla_finish(st, wuv_ref, tp)


def _idx_sample_kernel(pt_ref, iq_ref, iw_ref, new_ref, cache_ref, key_ref, buf, sem, *,
                       layer, n_pages, group):
    tp = SAMPLE_TP
    pager = _Pager(pt_ref, cache_ref, buf, sem, layer, n_pages, group, keys_minor=True)
    iq = _idx_queries(iq_ref[0])
    iwb = _idx_weights(iw_ref[0], tp)
    key = _sortable(_idx_score(iq, iwb, new_ref[0], tp))
    s_tot = (n_pages + 1) * PAGE
    key_ref[0, :, pl.ds(n_pages * PAGE, PAGE)] = jnp.where(_sample_vis(1, False), key, INT_MIN)
    if key_ref.shape[2] > s_tot:
        key_ref[0, :, pl.ds(s_tot, key_ref.shape[2] - s_tot)] = jnp.full((tp, key_ref.shape[2] - s_tot), INT_MIN, I32)

    def body(g, rows_ref, c):
        base = pl.multiple_of(pager.first_page(g) * PAGE, PAGE)
        key_ref[0, :, pl.ds(base, group * PAGE)] = _sortable(_idx_score(iq, iwb, rows_ref[...], tp))
        return c

    pager.sweep(body, 0)


def _thr_sample_kernel(key_ref, thr_ref, jl_ref, *, nchunks, cw, n_sel, idx_bits, t_real):
    rows = key_ref.shape[0]
    real = (lax.broadcasted_iota(I32, (rows, 1), 0) & (SAMPLE_TP - 1)) < t_real
    thr, jl = _select_threshold(key_ref, nchunks, cw, n_sel, idx_bits, real)
    thr_ref[...] = jnp.broadcast_to(thr, thr_ref.shape)
    jl_ref[...] = jnp.broadcast_to(jl, jl_ref.shape)


def _dsa_sample_kernel(pt_ref, q_ref, key_ref, thr_ref, jl_ref, new_ref, cache_ref, o_ref, buf, sem, *,
                       layer, n_pages, group):
    tp = SAMPLE_TP
    pager = _Pager(pt_ref, cache_ref, buf, sem, layer, n_pages, group)
    q = (_heads_to_rows(q_ref[0], HEAD_DIM, [h * HEAD_DIM for h in range(N_HEADS)])
         * (HEAD_DIM ** -0.5)).astype(BF16)
    thr = thr_ref[0][:, :1]
    jl = jl_ref[0][:, :1]

    def att(kv, base, n, st):
        bias = _dsa_bias(key_ref[0, :, pl.ds(base, n)], thr, jl, base, N_HEADS)
        s = _dot_nt(q, kv[:, :HEAD_DIM].astype(BF16)) + bias
        return _sm_block(s, kv[:, HEAD_DIM:].astype(BF16), *st)

    st = att(new_ref[0], n_pages * PAGE, PAGE, _sm_init(N_HEADS * tp, HEAD_DIM))

    def body(g, rows_ref, st):
        base = pl.multiple_of(pager.first_page(g) * PAGE, PAGE)
        return att(rows_ref[...], base, group * PAGE, st)

    st = pager.sweep(body, st)
    o_ref[0] = _rows_to_heads(st[2] * (1.0 / st[1]), N_HEADS, tp)


def _paged_call(kern, page_table, pre, cache, out_width, out_dtype, name, keys_minor=False):
    nb = page_table.shape[0]
    specs = []
    for kind, a in pre:
        if kind == "smem":
            specs.append(pl.BlockSpec(memory_space=pltpu.SMEM))
        elif kind == "seq":
            specs.append(pl.BlockSpec((1,) + a.shape[1:], lambda b, pt: (b, 0, 0)))
        else:
            specs.append(pl.BlockSpec(a.shape, lambda b, pt, n=a.ndim: (0,) * n))
    specs.append(pl.BlockSpec(memory_space=pl.ANY))
    pre = [a for _, a in pre]
    group = _largest_divisor(page_table.shape[1], PAGES_PER_GROUP)
    buf_shape = (2, cache.shape[2], group * PAGE) if keys_minor else (2, group * PAGE, cache.shape[3])
    return pl.pallas_call(
        functools.partial(kern, n_pages=page_table.shape[1], group=group),
        grid_spec=pltpu.PrefetchScalarGridSpec(
            num_scalar_prefetch=1, grid=(nb,), in_specs=specs,
            out_specs=pl.BlockSpec((1, SAMPLE_TP, out_width), lambda b, pt: (b, 0, 0)),
            scratch_shapes=[pltpu.VMEM(buf_shape, F32), pltpu.SemaphoreType.DMA((2,))]),
        out_shape=jax.ShapeDtypeStruct((nb, SAMPLE_TP, out_width), out_dtype),
        compiler_params=_cparams("arbitrary"), name=name,
    )(page_table, *pre, cache)


def _pad_rows(x, n):
    return jnp.pad(x, ((0, 0), (0, n - x.shape[1]), (0, 0)))


def sb_sample(page_table, q, new, cache, layer):
    kern = functools.partial(_sb_sample_kernel, layer=layer)
    return _paged_call(kern, page_table, [("seq", q), ("seq", new)], cache, BRANCH_WIDTH, F32, "sb_sample")


def diff_sample(page_table, lam, q, new, g, cache, layer, post_scale):
    kern = functools.partial(_diff_sample_kernel, layer=layer, post_scale=post_scale)
    pre = [("smem", lam), ("seq", q), ("seq", new), ("full", g)]
    return _paged_call(kern, page_table, pre, cache, BRANCH_WIDTH, F32, "diff_sample")


def mla_sample(page_table, ql, qp, new, wuv, cache, layer):
    kern = functools.partial(_mla_sample_kernel, layer=layer)
    pre = [("seq", ql), ("seq", qp), ("seq", new), ("full", wuv)]
    return _paged_call(kern, page_table, pre, cache, BRANCH_WIDTH, F32, "mla_sample", keys_minor=True)


def dsa_sample(page_table, iq, iw, new_idx, cache_idx, q, new_kv, cache_kv, layer, t_real):
    nb, n_pages = page_table.shape
    s_tot = n_pages * PAGE + PAGE
    cw = min(16 * LANES, s_tot)
    s_pad = -(-s_tot // cw) * cw
    kern = functools.partial(_idx_sample_kernel, layer=layer)
    pre = [("seq", iq), ("seq", iw), ("seq", new_idx)]
    keys = _paged_call(kern, page_table, pre, cache_idx, s_pad, I32, "idx_sample", keys_minor=True)
    rows = nb * SAMPLE_TP
    rb = _pick(rows, (64, 32, 16, 8))
    n_sel = min(TOPK_MAX, (n_pages * PAGE + t_real) // 4)
    thr, jl = pl.pallas_call(
        functools.partial(_thr_sample_kernel, nchunks=s_pad // cw, cw=cw, n_sel=n_sel,
                          idx_bits=int(s_tot).bit_length(), t_real=t_real),
        grid=(rows // rb,),
        in_specs=[pl.BlockSpec((rb, s_pad), lambda i: (i, 0))],
        out_specs=[pl.BlockSpec((rb, LANES), lambda i: (i, 0))] * 2,
        out_shape=[jax.ShapeDtypeStruct((rows, LANES), I32)] * 2,
        compiler_params=_cparams("parallel"), name="thr_sample",
    )(keys.reshape(rows, s_pad))
    kern = functools.partial(_dsa_sample_kernel, layer=layer)
    pre = [("seq", q), ("seq", keys), ("seq", thr.reshape(nb, SAMPLE_TP, LANES)),
           ("seq", jl.reshape(nb, SAMPLE_TP, LANES)), ("seq", new_kv)]
    return _paged_call(kern, page_table, pre, cache_kv, BRANCH_WIDTH, F32, "dsa_sample")


def _rms_matmul_kernel(x_ref, g_ref, w_ref, o_ref):
    o_ref[...] = _dot(_rms(x_ref[...], g_ref[...]).astype(BF16), w_ref[...])


def rms_matmul(x, g, w):
    n, d = x.shape
    m = w.shape[1]
    tm = _pick(n, (512, 256, 128, 64, 32, 16, 8))
    tn = _pick(m, (640, 512, 384, 256, 128))
    return pl.pallas_call(
        _rms_matmul_kernel, grid=(n // tm, m // tn),
        in_specs=[pl.BlockSpec((tm, d), lambda i, j: (i, 0)), pl.BlockSpec((1, d), lambda i, j: (0, 0)),
                  pl.BlockSpec((d, tn), lambda i, j: (0, j))],
        out_specs=pl.BlockSpec((tm, tn), lambda i, j: (i, j)),
        out_shape=jax.ShapeDtypeStruct((n, m), F32),
        compiler_params=_cparams("parallel", "arbitrary"), name="rms_matmul",
    )(x, g.reshape(1, d), w)


def _matmul_kernel(x_ref, w_ref, o_ref):
    o_ref[...] = _dot(x_ref[...].astype(BF16), w_ref[...])


def matmul(x, w):
    n, d = x.shape
    m = w.shape[1]
    tm = _pick(n, (512, 256, 128, 64, 32, 16, 8))
    return pl.pallas_call(
        _matmul_kernel, grid=(n // tm,),
        in_specs=[pl.BlockSpec((tm, d), lambda i: (i, 0)), pl.BlockSpec((d, m), lambda i: (0, 0))],
        out_specs=pl.BlockSpec((tm, m), lambda i: (i, 0)),
        out_shape=jax.ShapeDtypeStruct((n, m), F32),
        compiler_params=_cparams("parallel"), name="matmul",
    )(x, w)


def _merge_kernel(h_ref, g_ref, outs_ref, wgate_ref, wbr_ref, wout_ref, o_ref):
    h = h_ref[...]
    d = h.shape[1]
    a = _rms(h, g_ref[...]).astype(BF16)
    outs = outs_ref[...].astype(BF16)
    merged = None
    for n in range(N_BRANCH):
        gate = jax.nn.sigmoid(_dot(a, wgate_ref[:, n * d:(n + 1) * d]))
        y = _dot(outs[:, n * BRANCH_WIDTH:(n + 1) * BRANCH_WIDTH], wbr_ref[n])
        merged = gate * y if merged is None else merged + gate * y
    o_ref[...] = h + _dot(merged.astype(BF16), wout_ref[...])


def merge(h, g, outs, w_gate, w_branch, w_out):
    n, d = h.shape
    tm = _pick(n, (256, 128, 64, 32, 16, 8))
    full = lambda a: pl.BlockSpec(a.shape, lambda i, k=a.ndim: (0,) * k)
    return pl.pallas_call(
        _merge_kernel, grid=(n // tm,),
        in_specs=[pl.BlockSpec((tm, d), lambda i: (i, 0)), pl.BlockSpec((1, d), lambda i: (0, 0)),
                  pl.BlockSpec((tm, outs.shape[1]), lambda i: (i, 0)), full(w_gate), full(w_branch), full(w_out)],
        out_specs=pl.BlockSpec((tm, d), lambda i: (i, 0)),
        out_shape=jax.ShapeDtypeStruct((n, d), F32),
        compiler_params=_cparams("parallel"), name="merge",
    )(h, g.reshape(1, d), outs, w_gate, w_branch, w_out)


def _ffn_kernel(h_ref, g_ref, wg_ref, wu_ref, wd_ref, o_ref, c_scr, acc_scr):
    f = pl.program_id(1)

    @pl.when(f == 0)
    def _():
        c_scr[...] = _rms(h_ref[...], g_ref[...]).astype(BF16)
        acc_scr[...] = jnp.zeros_like(acc_scr)

    c = c_scr[...]
    hid = jax.nn.silu(_dot(c, wg_ref[...])) * _dot(c, wu_ref[...])
    acc_scr[...] += _dot(hid.astype(BF16), wd_ref[...])

    @pl.when(f == pl.num_programs(1) - 1)
    def _():
        o_ref[...] = h_ref[...] + acc_scr[...]


def ffn(h, g, wg, wu, wd):
    n, d = h.shape
    ff = wg.shape[1]
    tm = _pick(n, (1024, 512, 256, 128, 64, 32, 16, 8))
    tf = LANES * _largest_divisor(ff // LANES, 4)
    return pl.pallas_call(
        _ffn_kernel, grid=(n // tm, ff // tf),
        in_specs=[pl.BlockSpec((tm, d), lambda i, f: (i, 0)), pl.BlockSpec((1, d), lambda i, f: (0, 0)),
                  pl.BlockSpec((d, tf), lambda i, f: (0, f)), pl.BlockSpec((d, tf), lambda i, f: (0, f)),
                  pl.BlockSpec((tf, d), lambda i, f: (f, 0))],
        out_specs=pl.BlockSpec((tm, d), lambda i, f: (i, 0)),
        out_shape=jax.ShapeDtypeStruct((n, d), F32),
        scratch_shapes=[pltpu.VMEM((tm, d), BF16), pltpu.VMEM((tm, d), F32)],
        compiler_params=_cparams("parallel", "arbitrary"), name="ffn",
    )(h, g.reshape(1, d), wg, wu, wd)


def _split_bf16(x):
    hi = x.astype(BF16)
    return hi, (x - hi.astype(F32)).astype(BF16)


def _moe_kernel(h_ref, g_ref, wr_hi_ref, wr_lo_ref, wg_ref, wu_ref, wd_ref, o_ref, c_scr, comb_scr, acc_scr):
    e = pl.program_id(1)
    f = pl.program_id(2)
    lane = lax.broadcasted_iota(I32, (1, LANES), 1)

    @pl.when((e == 0) & (f == 0))
    def _():
        c = _rms(h_ref[...], g_ref[...])
        c_scr[...] = c.astype(BF16)
        c_hi, c_lo = _split_bf16(c)
        logits = _dot(c_hi, wr_hi_ref[...]) + (_dot(c_hi, wr_lo_ref[...]) + _dot(c_lo, wr_hi_ref[...]))
        logits = jnp.where(lane < N_EXPERTS, logits, -jnp.inf)
        m1 = jnp.max(logits, axis=1, keepdims=True)
        i1 = jnp.min(jnp.where(logits == m1, lane, LANES), axis=1, keepdims=True)
        rest = jnp.where(lane == i1, -jnp.inf, logits)
        m2 = jnp.max(rest, axis=1, keepdims=True)
        i2 = jnp.min(jnp.where(rest == m2, lane, LANES), axis=1, keepdims=True)
        r = jnp.exp(m2 - m1)
        w1 = 1.0 / (1.0 + r)
        comb_scr[...] = jnp.where(lane == i1, w1, 0.0) + jnp.where(lane == i2, r * w1, 0.0)
        acc_scr[...] = jnp.zeros_like(acc_scr)

    c = c_scr[...]
    ce = jnp.sum(jnp.where(lane == e, comb_scr[...], 0.0), axis=1, keepdims=True)
    hid = jax.nn.silu(_dot(c, wg_ref[...])) * _dot(c, wu_ref[...]) * ce
    acc_scr[...] += _dot(hid.astype(BF16), wd_ref[...])

    @pl.when((e == pl.num_programs(1) - 1) & (f == pl.num_programs(2) - 1))
    def _():
        o_ref[...] = h_ref[...] + acc_scr[...]


def moe(h, g, w_router, wg, wu, wd):
    n, d = h.shape
    ne, _, ff = wg.shape
    tm = _pick(n, (1024, 512, 256, 128, 64, 32, 16, 8))
    tf = LANES * _largest_divisor(ff // LANES, 4)
    wr = jnp.pad(w_router, ((0, 0), (0, LANES - ne)))
    wr_hi = wr.astype(BF16)
    wr_lo = (wr - wr_hi.astype(F32)).astype(BF16)
    return pl.pallas_call(
        _moe_kernel, grid=(n // tm, ne, ff // tf),
        in_specs=[pl.BlockSpec((tm, d), lambda i, e, f: (i, 0)), pl.BlockSpec((1, d), lambda i, e, f: (0, 0)),
                  pl.BlockSpec((d, LANES), lambda i, e, f: (0, 0)), pl.BlockSpec((d, LANES), lambda i, e, f: (0, 0)),
                  pl.BlockSpec((None, d, tf), lambda i, e, f: (e, 0, f)),
                  pl.BlockSpec((None, d, tf), lambda i, e, f: (e, 0, f)),
                  pl.BlockSpec((None, tf, d), lambda i, e, f: (e, f, 0))],
        out_specs=pl.BlockSpec((tm, d), lambda i, e, f: (i, 0)),
        out_shape=jax.ShapeDtypeStruct((n, d), F32),
        scratch_shapes=[pltpu.VMEM((tm, d), BF16), pltpu.VMEM((tm, LANES), F32), pltpu.VMEM((tm, d), F32)],
        compiler_params=_cparams("parallel", "arbitrary", "arbitrary"), name="moe",
    )(h, g.reshape(1, d), wr_hi, wr_lo, wg, wu, wd)


def _ple_kernel(h_ref, g_ref, p_ref, wgate_ref, wple_ref, o_ref):
    h = h_ref[...]
    gate = jax.nn.sigmoid(_dot(_rms(h, g_ref[...]).astype(BF16), wgate_ref[...]))
    o_ref[...] = h + gate * _dot(p_ref[...].astype(BF16), wple_ref[...])


def ple(h, g, p, w_gate, w_ple):
    n, d = h.shape
    tm = _pick(n, (512, 256, 128, 64, 32, 16, 8))
    full = lambda a: pl.BlockSpec(a.shape, lambda i, k=a.ndim: (0,) * k)
    return pl.pallas_call(
        _ple_kernel, grid=(n // tm,),
        in_specs=[pl.BlockSpec((tm, d), lambda i: (i, 0)), pl.BlockSpec((1, d), lambda i: (0, 0)),
                  pl.BlockSpec((tm, p.shape[1]), lambda i: (i, 0)), full(w_gate), full(w_ple)],
        out_specs=pl.BlockSpec((tm, d), lambda i: (i, 0)),
        out_shape=jax.ShapeDtypeStruct((n, d), F32),
        compiler_params=_cparams("parallel"), name="ple",
    )(h, g.reshape(1, d), p, w_gate, w_ple)


def _rmsnorm(x, g):
    return x * lax.rsqrt(jnp.mean(x * x, axis=-1, keepdims=True) + RMS_EPS) * g


def _rope(x, pos):
    half = x.shape[-1] // 2
    inv_freq = ROPE_THETA ** (-jnp.arange(half, dtype=F32) / half)
    ang = pos.astype(F32)[:, None] * inv_freq[None, :]
    ang = ang.reshape((ang.shape[0],) + (1,) * (x.ndim - 2) + (half,))
    cos, sin = jnp.cos(ang), jnp.sin(ang)
    x1, x2 = x[..., :half], x[..., half:]
    return jnp.concatenate([x1 * cos - x2 * sin, x1 * sin + x2 * cos], axis=-1)


def _block_diag_uk(w_uk):
    c, h, e = w_uk.shape
    out = jnp.zeros((h * e, h * c), w_uk.dtype)
    for i in range(h):
        out = out.at[i * e:(i + 1) * e, i * c:(i + 1) * c].set(w_uk[:, i, :].T)
    return out


def _token_features(z, pos, w, i):
    n = z.shape[0]
    h = N_HEADS
    split_at = [int(v) for v in np.cumsum(IN_SPLITS)[:-1]]
    (sb_q, sb_k, sb_v, df_q, df_k, df_v, m_cq, m_ckv, m_kpe,
     ds_q, ds_k, ds_v, ix_q, ix_k, ix_w) = jnp.split(z[:, :IN_COLS], split_at, axis=-1)
    f = {}
    f["sb_q"] = sb_q
    f["sb_row"] = jnp.concatenate([sb_k, sb_v], axis=-1)
    df_q = _rope(_rmsnorm(df_q.reshape(n, h, 2, DIFF_QK), w["diff_q_norm"][i]), pos)
    df_k = _rope(_rmsnorm(df_k.reshape(n, 2, DIFF_QK), w["diff_k_norm"][i]), pos)
    f["df_q"] = df_q.reshape(n, h * 2 * DIFF_QK)
    f["diff_row"] = jnp.concatenate([df_k.reshape(n, 2 * DIFF_QK), df_v], axis=-1)
    w_uq = w["w_mla_uq"][i].reshape(MLA_Q_RANK, h * (MLA_NOPE + MLA_ROPE)).astype(BF16)
    q_full = matmul(_rmsnorm(m_cq, w["mla_q_norm_a"][i]), w_uq).reshape(n, h, MLA_NOPE + MLA_ROPE)
    q_nope = _rmsnorm(q_full[..., :MLA_NOPE], w["mla_qn_nope"][i])
    q_pe = _rope(_rmsnorm(q_full[..., MLA_NOPE:], w["mla_qn_pe"][i]), pos)
    f["q_lat"] = matmul(q_nope.reshape(n, h * MLA_NOPE), _block_diag_uk(w["w_mla_uk"][i]).astype(BF16))
    f["q_pe"] = q_pe.reshape(n, h * MLA_ROPE)
    f["mla_row"] = jnp.concatenate([_rmsnorm(m_ckv, w["mla_kv_norm"][i]),
                                    _rope(_rmsnorm(m_kpe, w["mla_kn_pe"][i]), pos)], axis=-1)
    f["ds_q"] = _rope(_rmsnorm(ds_q.reshape(n, h, HEAD_DIM), w["dsa_q_norm"][i]), pos).reshape(n, h * HEAD_DIM)
    f["ds_row"] = jnp.concatenate([_rope(_rmsnorm(ds_k, w["dsa_k_norm"][i]), pos), ds_v], axis=-1)
    f["ix_q"] = _rope(ix_q.reshape(n, IDX_HEADS, IDX_DIM), pos).reshape(n, IDX_HEADS * IDX_DIM)
    f["idx_row"] = _rope(ix_k, pos)
    f["ix_w"] = ix_w
    return f


def kernel(x_prompt, x_sample, cache_sb, cache_diff, cache_mla, cache_dsa_kv, cache_dsa_idx, page_table, p_prompt, p_sample, norm_mix, w_in, diff_q_norm, diff_k_norm, diff_lq1, diff_lk1, diff_lq2, diff_lk2, diff_subln, mla_q_norm_a, mla_kv_norm, w_mla_uq, w_mla_uk, w_mla_uv, mla_qn_nope, mla_qn_pe, mla_kn_pe, dsa_q_norm, dsa_k_norm, w_branch, w_gate, w_out, norm_ffn, w_ff_gate, w_ff_up, w_ff_down, w_router, w_moe_gate, w_moe_up, w_moe_down, norm_ple, w_ple_gate, w_ple):
    w = dict(diff_q_norm=diff_q_norm, diff_k_norm=diff_k_norm, mla_q_norm_a=mla_q_norm_a,
             mla_kv_norm=mla_kv_norm, w_mla_uq=w_mla_uq, w_mla_uk=w_mla_uk, mla_qn_nope=mla_qn_nope,
             mla_qn_pe=mla_qn_pe, mla_kn_pe=mla_kn_pe, dsa_q_norm=dsa_q_norm, dsa_k_norm=dsa_k_norm)
    bp, tp, d = x_prompt.shape
    bs, ts, _ = x_sample.shape
    depth = w_in.shape[0]
    n_pages = page_table.shape[1]
    past_len = n_pages * PAGE
    npr, nsm = bp * tp, bs * ts
    n_tok = npr + nsm
    n_pad = -n_tok % 256
    pos = jnp.concatenate([jnp.tile(jnp.arange(tp, dtype=I32), bp),
                           jnp.tile(past_len + jnp.arange(ts, dtype=I32), bs),
                           jnp.zeros((n_pad,), I32)])
    h = jnp.concatenate([x_prompt.reshape(npr, d), x_sample.reshape(nsm, d), jnp.zeros((n_pad, d), F32)])
    in_pad = -IN_COLS % LANES
    cache_mla_t = jnp.swapaxes(cache_mla, 2, 3)
    cache_idx_t = jnp.swapaxes(cache_dsa_idx, 2, 3)
    names = ("sb_row", "diff_row", "mla_row", "ds_row", "idx_row")
    new_rows = {k: [] for k in names}

    for i in range(depth):
        z = rms_matmul(h, norm_mix[i], jnp.pad(w_in[i], ((0, 0), (0, in_pad))).astype(BF16))
        f = _token_features(z, pos, w, i)
        pr = lambda a: a[:npr].reshape(bp, tp, a.shape[-1])
        sm = lambda a: a[npr:n_tok].reshape(bs, ts, a.shape[-1])
        smq = lambda a: _pad_rows(sm(a), SAMPLE_TP)
        smk = lambda a: _pad_rows(sm(a), PAGE)
        for k in names:
            new_rows[k].append((pr(f[k]), sm(f[k])))

        lam_init = 0.8 - 0.6 * math.exp(-0.3 * i)
        lam = (jnp.exp(jnp.sum(diff_lq1[i] * diff_lk1[i])) - jnp.exp(jnp.sum(diff_lq2[i] * diff_lk2[i]))
               + lam_init).reshape(1).astype(F32)
        subln = diff_subln[i].reshape(1, HEAD_DIM)
        wuv = jnp.transpose(w_mla_uv[i], (1, 0, 2)).astype(BF16)

        outs_p = [sb_prompt(pr(f["sb_q"]), pr(f["sb_row"])),
                  diff_prompt(lam, pr(f["df_q"]), pr(f["diff_row"]), subln, 1.0 - lam_init),
                  mla_prompt(pr(f["q_lat"]), pr(f["q_pe"]), pr(f["mla_row"]), wuv),
                  dsa_prompt(pr(f["ix_q"]), pr(f["ix_w"]), jnp.swapaxes(pr(f["idx_row"]), 1, 2),
                             pr(f["ds_q"]), pr(f["ds_row"]))]
        smkt = lambda a: jnp.swapaxes(smk(a), 1, 2)
        outs_s = [sb_sample(page_table, smq(f["sb_q"]), smk(f["sb_row"]), cache_sb, i),
                  diff_sample(page_table, lam, smq(f["df_q"]), smk(f["diff_row"]), subln, cache_diff, i,
                              1.0 - lam_init),
                  mla_sample(page_table, smq(f["q_lat"]), smq(f["q_pe"]), smkt(f["mla_row"]), wuv, cache_mla_t, i),
                  dsa_sample(page_table, smq(f["ix_q"]), smq(f["ix_w"]), smkt(f["idx_row"]), cache_idx_t,
                             smq(f["ds_q"]), smk(f["ds_row"]), cache_dsa_kv, i, ts)]
        outs = jnp.concatenate([
            jnp.concatenate([o.reshape(npr, BRANCH_WIDTH) for o in outs_p], axis=1),
            jnp.concatenate([o[:, :ts].reshape(nsm, BRANCH_WIDTH) for o in outs_s], axis=1),
            jnp.zeros((n_pad, N_BRANCH * BRANCH_WIDTH), F32)])
        h = merge(h, norm_mix[i], outs, w_gate[i].astype(BF16), w_branch[i].astype(BF16), w_out[i].astype(BF16))
        j = i // 2
        if i % 2 == 0:
            h = ffn(h, norm_ffn[i], w_ff_gate[j].astype(BF16), w_ff_up[j].astype(BF16), w_ff_down[j].astype(BF16))
        else:
            h = moe(h, norm_ffn[i], w_router[j], w_moe_gate[j].astype(BF16), w_moe_up[j].astype(BF16),
                    w_moe_down[j].astype(BF16))
        p = jnp.concatenate([p_prompt[i].reshape(npr, -1), p_sample[i].reshape(nsm, -1),
                             jnp.zeros((n_pad, p_prompt.shape[-1]), F32)])
        h = ple(h, norm_ple[i], p, w_ple_gate[i].astype(BF16), w_ple[i].astype(BF16))

    y_prompt = h[:npr].reshape(bp, tp, d)
    y_sample = h[npr:n_tok].reshape(bs, ts, d)
    state = []
    for k in names:
        state.append(jnp.stack([r[0] for r in new_rows[k]], axis=0))
        state.append(jnp.stack([r[1] for r in new_rows[k]], axis=0))
    return (y_prompt, y_sample) + tuple(state)
```

```python
import functools
import math

import numpy as np
import jax
import jax.numpy as jnp
from jax import lax
from jax.experimental import pallas as pl
from jax.experimental.pallas import tpu as pltpu

F32, BF16, I32 = jnp.float32, jnp.bfloat16, jnp.int32

N_HEADS = 4
HEAD_DIM = 64
N_BRANCH = 4
BRANCH_WIDTH = N_HEADS * HEAD_DIM
DIFF_QK = 32
MLA_Q_RANK = 256
MLA_KV_RANK = 128
MLA_NOPE = 64
MLA_ROPE = 32
IDX_HEADS = 8
IDX_DIM = 32
TOPK_MAX = 256
N_EXPERTS = 8
ROPE_THETA = 10000.0
RMS_EPS = 1e-6
PAGE = 128
IN_SPLITS = (
    N_HEADS * HEAD_DIM, HEAD_DIM, HEAD_DIM,
    N_HEADS * 2 * DIFF_QK, 2 * DIFF_QK, HEAD_DIM,
    MLA_Q_RANK, MLA_KV_RANK, MLA_ROPE,
    N_HEADS * HEAD_DIM, HEAD_DIM, HEAD_DIM,
    IDX_HEADS * IDX_DIM, IDX_DIM, IDX_HEADS,
)
IN_COLS = sum(IN_SPLITS)

LANES = 128
SUBLANES = 8
VMEM_LIMIT = 52 * 1024 * 1024
NEG = -1e30
INT_MIN = -2 ** 31
IDX_BIG = 2 ** 30
SAMPLE_TP = SUBLANES
PAGES_PER_GROUP = 64


def _cparams(*sem):
    return pltpu.CompilerParams(dimension_semantics=sem, vmem_limit_bytes=VMEM_LIMIT)


def _dot(a, b):
    return jnp.dot(a, b, preferred_element_type=F32)


def _dot_nt(a, b):
    return lax.dot_general(a, b, (((1,), (1,)), ((), ())), preferred_element_type=F32)


def _pick(n, cands):
    for c in cands:
        if n % c == 0:
            return c
    return n


def _largest_divisor(n, maxd):
    for d in range(min(n, maxd), 0, -1):
        if n % d == 0:
            return d
    return 1


def _rms(x, g):
    return x * lax.rsqrt(jnp.mean(x * x, axis=-1, keepdims=True) + RMS_EPS) * g


def _heads_to_rows(x, width, starts):
    return jnp.concatenate([x[:, s:s + width] for s in starts], axis=0)


def _rows_to_heads(x, n, t):
    return jnp.concatenate([x[h * t:(h + 1) * t] for h in range(n)], axis=1)


def _tri(n):
    return (lax.broadcasted_iota(I32, (n, n), 0) >= lax.broadcasted_iota(I32, (n, n), 1)).astype(BF16)


def _sb_block(q, kv, carry, acc, tri, vis):
    n, ck = kv.shape[0], tri.shape[0]
    k = kv[:, :HEAD_DIM].astype(BF16)
    v = kv[:, HEAD_DIM:].astype(BF16)
    z = _dot_nt(q, k)
    t = jnp.log1p(jnp.exp(-jnp.abs(z)))
    log_keep = -(jnp.maximum(z, 0.0) + t)
    log_beta = jnp.minimum(z, 0.0) - t
    if vis is not None:
        log_keep = jnp.where(vis, log_keep, 0.0)
    incl = []
    for c in range(n // ck):
        lk = log_keep[:, c * ck:(c + 1) * ck]
        hi = lk.astype(BF16)
        lo = (lk - hi.astype(F32)).astype(BF16)
        incl.append(_dot(hi, tri) + _dot(lo, tri))
    later = [None] * len(incl)
    for c in reversed(range(len(incl))):
        later[c] = incl[c] + carry
        carry = carry + incl[c][:, :1]
    w = jnp.exp(log_beta + (jnp.concatenate(later, axis=1) - log_keep))
    if vis is not None:
        w = jnp.where(vis, w, 0.0)
    return carry, acc + _dot(w.astype(BF16), v)


def _sm_block(s, v, m, l, acc, v_keys_minor=False):
    m_new = jnp.maximum(m, jnp.max(s, axis=1, keepdims=True))
    alpha = jnp.exp(m - m_new)
    p = jnp.exp(s - m_new)
    l = alpha * l + jnp.sum(p, axis=1, keepdims=True)
    pv = _dot_nt(p.astype(BF16), v) if v_keys_minor else _dot(p.astype(BF16), v)
    return m_new, l, alpha * acc + pv


def _sm_init(rows, width):
    return (jnp.full((rows, 1), NEG, F32), jnp.zeros((rows, 1), F32), jnp.zeros((rows, width), F32))


def _diff_queries(x):
    return [_heads_to_rows(x, DIFF_QK, [h * 2 * DIFF_QK + c * DIFF_QK for h in range(N_HEADS)]).astype(BF16)
            for c in range(2)]


def _diff_step(qs, kv, st, vis):
    v = kv[:, 2 * DIFF_QK:].astype(BF16)
    out = []
    for c in range(2):
        k = kv[:, c * DIFF_QK:(c + 1) * DIFF_QK].astype(BF16)
        s = _dot_nt(qs[c], k) * (DIFF_QK ** -0.5)
        if vis is not None:
            s = jnp.where(vis, s, NEG)
        out.extend(_sm_block(s, v, *st[3 * c:3 * c + 3]))
    return tuple(out)


def _diff_finish(st, lam, g, post_scale, t):
    o = st[2] * (1.0 / st[1]) - lam * (st[5] * (1.0 / st[4]))
    o = _rms(o, g) * post_scale
    return _rows_to_heads(o, N_HEADS, t)


def _mla_queries(xl, xp):
    ql = _heads_to_rows(xl, MLA_KV_RANK, [h * MLA_KV_RANK for h in range(N_HEADS)]).astype(BF16)
    qp = _heads_to_rows(xp, MLA_ROPE, [h * MLA_ROPE for h in range(N_HEADS)]).astype(BF16)
    return ql, qp


def _mla_step(ql, qp, kv, st, vis, keys_minor=False):
    if keys_minor:
        c = kv[:MLA_KV_RANK].astype(BF16)
        s = _dot(ql, c) + _dot(qp, kv[MLA_KV_RANK:].astype(BF16))
    else:
        c = kv[:, :MLA_KV_RANK].astype(BF16)
        s = _dot_nt(ql, c) + _dot_nt(qp, kv[:, MLA_KV_RANK:].astype(BF16))
    s = s * ((MLA_NOPE + MLA_ROPE) ** -0.5)
    if vis is not None:
        s = jnp.where(vis, s, NEG)
    return _sm_block(s, c, *st, v_keys_minor=keys_minor)


def _mla_finish(st, wuv_ref, t):
    lat = (st[2] * (1.0 / st[1])).astype(BF16)
    return jnp.concatenate([_dot(lat[h * t:(h + 1) * t], wuv_ref[h]) for h in range(N_HEADS)], axis=1)


def _idx_queries(x):
    return _heads_to_rows(x, IDX_DIM, [h * IDX_DIM for h in range(IDX_HEADS)]).astype(BF16)


def _idx_weights(iw, t):
    return [jnp.broadcast_to(iw[:, h:h + 1], (t, LANES)) for h in range(IDX_HEADS)]


def _idx_score(iq, iwb, kxt, t):
    r = _dot(iq, kxt.astype(BF16))
    tiles = []
    for j in range(kxt.shape[1] // LANES):
        sl = slice(j * LANES, (j + 1) * LANES)
        sc = jnp.maximum(r[:t, sl], 0.0) * iwb[0]
        for h in range(1, IDX_HEADS):
            sc = sc + jnp.maximum(r[h * t:(h + 1) * t, sl], 0.0) * iwb[h]
        tiles.append(sc)
    return jnp.concatenate(tiles, axis=1)


def _sortable(x):
    x = jnp.where(x == 0.0, 0.0, x)
    b = pltpu.bitcast(x, I32)
    return b ^ ((b >> 31) & jnp.int32(0x7FFFFFFF))


def _count(ref, nchunks, cw, pred):
    rows = ref.shape[0]

    def body(c, acc):
        base = pl.multiple_of(c * cw, LANES)
        m = pred(ref[:, pl.ds(base, cw)], base).astype(I32)
        for j in range(cw // LANES):
            acc = acc + m[:, j * LANES:(j + 1) * LANES]
        return acc

    acc = lax.fori_loop(0, nchunks, body, jnp.zeros((rows, LANES), I32))
    return jnp.sum(acc, axis=1, keepdims=True)


def _select_threshold(ref, nchunks, cw, k, idx_bits, real_rows):
    rows = ref.shape[0]

    def bit_cond(c):
        i, _, settled = c
        return (i < 32) & (jnp.min(settled) == 0)

    def bit_body(c):
        i, u, settled = c
        cand_u = u | lax.shift_left(jnp.int32(1), 31 - i)
        cand = cand_u ^ jnp.int32(INT_MIN)
        cnt = _count(ref, nchunks, cw, lambda blk, base: blk >= cand)
        u = jnp.where((settled == 0) & (cnt >= k), cand_u, u)
        return i + 1, u, settled | (cnt == k).astype(I32)

    zeros = jnp.zeros((rows, 1), I32)
    settled0 = zeros if real_rows is None else 1 - real_rows.astype(I32)
    _, u, _ = lax.while_loop(bit_cond, bit_body, (jnp.int32(0), zeros, settled0))
    thr = u ^ jnp.int32(INT_MIN)
    cnt_gt = _count(ref, nchunks, cw, lambda blk, base: blk > thr)
    cnt_ge = _count(ref, nchunks, cw, lambda blk, base: blk >= thr)
    need = k - cnt_gt
    tie = (cnt_ge > k) & (thr > INT_MIN)
    if real_rows is not None:
        tie = tie & real_rows

    def tie_path():
        def jbody(i, j):
            cand = j | lax.shift_left(jnp.int32(1), idx_bits - 1 - i)

            def pred(blk, base):
                kid = base + lax.broadcasted_iota(I32, (1, cw), 1)
                return (blk == thr) & (kid < cand)

            cnt = _count(ref, nchunks, cw, pred)
            return jnp.where(cnt <= need, cand, j)

        return lax.fori_loop(0, idx_bits, jbody, jnp.zeros((rows, 1), I32))

    jl = lax.cond(jnp.max(tie.astype(I32)) > 0, tie_path, lambda: jnp.zeros((rows, 1), I32))
    jl = jnp.where(tie, jl, jnp.where(thr > INT_MIN, IDX_BIG, 0))
    return thr, jl


def _dsa_bias(key, thr, jl, base, heads):
    n = key.shape[1]
    kid = base + lax.broadcasted_iota(I32, (1, n), 1)
    sel = (key > thr) | ((key == thr) & (kid < jl))
    bias = jnp.where(sel, 0.0, NEG)
    return jnp.concatenate([bias] * heads, axis=0)


def _kblock(ref, kb, tq):
    return ref[0, pl.ds(pl.multiple_of(kb * tq, tq), tq), :]


def _causal_vis(qi, tq, kb, tk, heads, strict):
    qpos = qi * tq + (lax.broadcasted_iota(I32, (heads * tq, 1), 0) & (tq - 1))
    kpos = kb * tk + lax.broadcasted_iota(I32, (1, tk), 1)
    return (kpos < qpos) if strict else (kpos <= qpos)


def _sb_prompt_kernel(q_ref, kv_ref, o_ref, *, tq, tk):
    qi = pl.program_id(1)
    nfull = (qi * tq) // tk
    rows = N_HEADS * tq
    q = (_heads_to_rows(q_ref[0], HEAD_DIM, [h * HEAD_DIM for h in range(N_HEADS)])
         * (HEAD_DIM ** -0.5)).astype(BF16)
    tri = _tri(min(tk, 2 * LANES))
    st = (jnp.zeros((rows, 1), F32), jnp.zeros((rows, HEAD_DIM), F32))
    st = _sb_block(q, _kblock(kv_ref, nfull, tk), *st, tri, _causal_vis(qi, tq, nfull, tk, N_HEADS, True))
    st = lax.fori_loop(0, nfull, lambda i, c: _sb_block(q, _kblock(kv_ref, nfull - 1 - i, tk), *c, tri, None), st)
    o_ref[0] = _rows_to_heads(st[1], N_HEADS, tq)


def _diff_prompt_kernel(lam_ref, q_ref, kv_ref, g_ref, o_ref, *, tq, tk, post_scale):
    qi = pl.program_id(1)
    nfull = (qi * tq) // tk
    rows = N_HEADS * tq
    qs = _diff_queries(q_ref[0])
    st = _sm_init(rows, HEAD_DIM) * 2
    st = lax.fori_loop(0, nfull, lambda i, c: _diff_step(qs, _kblock(kv_ref, i, tk), c, None), st)
    st = _diff_step(qs, _kblock(kv_ref, nfull, tk), st, _causal_vis(qi, tq, nfull, tk, N_HEADS, False))
    o_ref[0] = _diff_finish(st, lam_ref[0], g_ref[...], post_scale, tq)


def _mla_prompt_kernel(ql_ref, qp_ref, kv_ref, wuv_ref, o_ref, *, tq, tk):
    qi = pl.program_id(1)
    nfull = (qi * tq) // tk
    rows = N_HEADS * tq
    ql, qp = _mla_queries(ql_ref[0], qp_ref[0])
    st = _sm_init(rows, MLA_KV_RANK)
    st = lax.fori_loop(0, nfull, lambda i, c: _mla_step(ql, qp, _kblock(kv_ref, i, tk), c, None), st)
    st = _mla_step(ql, qp, _kblock(kv_ref, nfull, tk), st, _causal_vis(qi, tq, nfull, tk, N_HEADS, False))
    o_ref[0] = _mla_finish(st, wuv_ref, tq)


def _dsa_prompt_kernel(iq_ref, iw_ref, kidxt_ref, q_ref, kv_ref, o_ref, key_scr, *, tq, tk, n_sel, idx_bits):
    qi = pl.program_id(1)
    nfull = (qi * tq) // tk
    iq = _idx_queries(iq_ref[0])
    iwb = _idx_weights(iw_ref[0], tq)

    def score(kb, vis):
        base = pl.multiple_of(kb * tk, tk)
        key = _sortable(_idx_score(iq, iwb, kidxt_ref[0, :, pl.ds(base, tk)], tq))
        if vis is not None:
            key = jnp.where(vis, key, INT_MIN)
        key_scr[:, pl.ds(base, tk)] = key

    def score_body(i, c):
        score(i, None)
        return c

    lax.fori_loop(0, nfull, score_body, 0)
    score(nfull, _causal_vis(qi, tq, nfull, tk, 1, False))
    count_blocks = 2 if key_scr.shape[1] % (2 * tk) == 0 else 1
    if count_blocks == 2:
        @pl.when(nfull % 2 == 0)
        def _():
            key_scr[:, pl.ds(pl.multiple_of((nfull + 1) * tk, tk), tk)] = jnp.full((tq, tk), INT_MIN, I32)

    thr, jl = _select_threshold(key_scr, (nfull + count_blocks) // count_blocks, count_blocks * tk,
                                n_sel, idx_bits, None)

    q = (_heads_to_rows(q_ref[0], HEAD_DIM, [h * HEAD_DIM for h in range(N_HEADS)])
         * (HEAD_DIM ** -0.5)).astype(BF16)

    def att(kb, st):
        kv = _kblock(kv_ref, kb, tk)
        base = pl.multiple_of(kb * tk, tk)
        bias = _dsa_bias(key_scr[:, pl.ds(base, tk)], thr, jl, base, N_HEADS)
        s = _dot_nt(q, kv[:, :HEAD_DIM].astype(BF16)) + bias
        return _sm_block(s, kv[:, HEAD_DIM:].astype(BF16), *st)

    st = lax.fori_loop(0, nfull + 1, att, _sm_init(N_HEADS * tq, HEAD_DIM))
    o_ref[0] = _rows_to_heads(st[2] * (1.0 / st[1]), N_HEADS, tq)


def _prompt_specs(tq, widths, t_total, kv_widths):
    qs = [pl.BlockSpec((1, tq, w), lambda b, i: (b, i, 0)) for w in widths]
    ks = [pl.BlockSpec((1, t_total, w), lambda b, i: (b, 0, 0)) for w in kv_widths]
    return qs, ks


def _prompt_tiles(t, tq_want, tk_want=512):
    tq = _pick(t, tuple(c for c in (256, 128, 64, 32, 16, 8) if c <= tq_want))
    tk = _pick(t, tuple(c for c in (1024, 512, 256, 128, 64, 32, 16, 8) if tq <= c <= tk_want))
    return tq, tk


def sb_prompt(q, kv):
    b, t, _ = q.shape
    tq, tk = _prompt_tiles(t, 256)
    qs, ks = _prompt_specs(tq, [BRANCH_WIDTH], t, [kv.shape[-1]])
    return pl.pallas_call(
        functools.partial(_sb_prompt_kernel, tq=tq, tk=tk),
        grid=(b, t // tq), in_specs=qs + ks,
        out_specs=pl.BlockSpec((1, tq, BRANCH_WIDTH), lambda b, i: (b, i, 0)),
        out_shape=jax.ShapeDtypeStruct((b, t, BRANCH_WIDTH), F32),
        compiler_params=_cparams("parallel", "arbitrary"), name="sb_prompt",
    )(q, kv)


def diff_prompt(lam, q, kv, g, post_scale):
    b, t, _ = q.shape
    tq, tk = _prompt_tiles(t, 256, 1024)
    qs, ks = _prompt_specs(tq, [BRANCH_WIDTH], t, [kv.shape[-1]])
    return pl.pallas_call(
        functools.partial(_diff_prompt_kernel, tq=tq, tk=tk, post_scale=post_scale),
        grid=(b, t // tq),
        in_specs=[pl.BlockSpec(memory_space=pltpu.SMEM)] + qs + ks
        + [pl.BlockSpec((1, HEAD_DIM), lambda b, i: (0, 0))],
        out_specs=pl.BlockSpec((1, tq, BRANCH_WIDTH), lambda b, i: (b, i, 0)),
        out_shape=jax.ShapeDtypeStruct((b, t, BRANCH_WIDTH), F32),
        compiler_params=_cparams("parallel", "arbitrary"), name="diff_prompt",
    )(lam, q, kv, g)


def mla_prompt(ql, qp, kv, wuv):
    b, t, _ = ql.shape
    tq, tk = _prompt_tiles(t, 256)
    qs, ks = _prompt_specs(tq, [ql.shape[-1], qp.shape[-1]], t, [kv.shape[-1]])
    return pl.pallas_call(
        functools.partial(_mla_prompt_kernel, tq=tq, tk=tk),
        grid=(b, t // tq),
        in_specs=qs + ks + [pl.BlockSpec(wuv.shape, lambda b, i: (0, 0, 0))],
        out_specs=pl.BlockSpec((1, tq, BRANCH_WIDTH), lambda b, i: (b, i, 0)),
        out_shape=jax.ShapeDtypeStruct((b, t, BRANCH_WIDTH), F32),
        compiler_params=_cparams("parallel", "arbitrary"), name="mla_prompt",
    )(ql, qp, kv, wuv)


def dsa_prompt(iq, iw, kidxt, q, kv):
    b, t, _ = q.shape
    tq, tk = _prompt_tiles(t, 256)
    n_sel = min(TOPK_MAX, t // 4)
    qs, _ = _prompt_specs(tq, [iq.shape[-1], iw.shape[-1]], t, [])
    ks = [pl.BlockSpec((1, kidxt.shape[1], t), lambda b, i: (b, 0, 0))]
    qs2, ks2 = _prompt_specs(tq, [BRANCH_WIDTH], t, [kv.shape[-1]])
    return pl.pallas_call(
        functools.partial(_dsa_prompt_kernel, tq=tq, tk=tk, n_sel=n_sel, idx_bits=int(t).bit_length()),
        grid=(b, t // tq),
        in_specs=qs + ks + qs2 + ks2,
        out_specs=pl.BlockSpec((1, tq, BRANCH_WIDTH), lambda b, i: (b, i, 0)),
        out_shape=jax.ShapeDtypeStruct((b, t, BRANCH_WIDTH), F32),
        scratch_shapes=[pltpu.VMEM((tq, t), I32)],
        compiler_params=_cparams("parallel", "arbitrary"), name="dsa_prompt",
    )(iq, iw, kidxt, q, kv)


class _Pager:
    def __init__(self, pt_ref, cache_ref, buf, sem, layer, n_pages, group, keys_minor=False):
        self.pt, self.cache, self.buf, self.sem = pt_ref, cache_ref, buf, sem
        self.layer, self.n_pages, self.group = layer, n_pages, group
        self.n_groups = n_pages // group
        self.keys_minor = keys_minor

    def _copy(self, page, slot, p):
        if self.keys_minor:
            dst = self.buf.at[slot, :, pl.ds(p * PAGE, PAGE)]
        else:
            dst = self.buf.at[slot, pl.ds(p * PAGE, PAGE)]
        return pltpu.make_async_copy(self.cache.at[self.layer, page], dst, self.sem.at[slot])

    def first_page(self, g):
        return self.n_pages - (g + 1) * self.group

    def start(self, b, g, slot):
        base = self.first_page(g)
        for p in range(self.group):
            self._copy(self.pt[b, base + p], slot, p).start()

    def wait(self, slot):
        for p in range(self.group):
            self._copy(0, slot, p).wait()

    def sweep(self, body, init):
        b = pl.program_id(0)
        nb = pl.num_programs(0)
        ng = self.n_groups

        @pl.when(b == 0)
        def _():
            self.start(0, 0, 0)

        def gbody(g, carry):
            slot = (b * ng + g) % 2

            @pl.when(g + 1 < ng)
            def _():
                self.start(b, g + 1, 1 - slot)

            @pl.when((g + 1 == ng) & (b + 1 < nb))
            def _():
                self.start(b + 1, 0, 1 - slot)

            self.wait(slot)
            return body(g, self.buf.at[slot], carry)

        return lax.fori_loop(0, ng, gbody, init)


def _sample_vis(heads, strict):
    row = lax.broadcasted_iota(I32, (heads * SAMPLE_TP, 1), 0) & (SAMPLE_TP - 1)
    col = lax.broadcasted_iota(I32, (1, PAGE), 1)
    return (col < row) if strict else (col <= row)


def _sb_sample_kernel(pt_ref, q_ref, new_ref, cache_ref, o_ref, buf, sem, *, layer, n_pages, group):
    tp = SAMPLE_TP
    rows = N_HEADS * tp
    pager = _Pager(pt_ref, cache_ref, buf, sem, layer, n_pages, group)
    q = (_heads_to_rows(q_ref[0], HEAD_DIM, [h * HEAD_DIM for h in range(N_HEADS)])
         * (HEAD_DIM ** -0.5)).astype(BF16)
    tri = _tri(min(2 * PAGE, group * PAGE))
    st = (jnp.zeros((rows, 1), F32), jnp.zeros((rows, HEAD_DIM), F32))
    st = _sb_block(q, new_ref[0], *st, _tri(PAGE), _sample_vis(N_HEADS, True))
    st = pager.sweep(lambda g, rows_ref, st: _sb_block(q, rows_ref[...], *st, tri, None), st)
    o_ref[0] = _rows_to_heads(st[1], N_HEADS, tp)


def _diff_sample_kernel(pt_ref, lam_ref, q_ref, new_ref, g_ref, cache_ref, o_ref, buf, sem, *,
                        layer, n_pages, group, post_scale):
    tp = SAMPLE_TP
    pager = _Pager(pt_ref, cache_ref, buf, sem, layer, n_pages, group)
    qs = _diff_queries(q_ref[0])
    st = _sm_init(N_HEADS * tp, HEAD_DIM) * 2
    st = _diff_step(qs, new_ref[0], st, _sample_vis(N_HEADS, False))
    st = pager.sweep(lambda g, rows_ref, st: _diff_step(qs, rows_ref[...], st, None), st)
    o_ref[0] = _diff_finish(st, lam_ref[0], g_ref[...], post_scale, tp)


def _mla_sample_kernel(pt_ref, ql_ref, qp_ref, new_ref, wuv_ref, cache_ref, o_ref, buf, sem, *,
                       layer, n_pages, group):
    tp = SAMPLE_TP
    pager = _Pager(pt_ref, cache_ref, buf, sem, layer, n_pages, group, keys_minor=True)
    ql, qp = _mla_queries(ql_ref[0], qp_ref[0])
    st = _sm_init(N_HEADS * tp, MLA_KV_RANK)
    st = _mla_step(ql, qp, new_ref[0], st, _sample_vis(N_HEADS, False), keys_minor=True)
    st = pager.sweep(lambda g, rows_ref, st: _mla_step(ql, qp, rows_ref[...], st, None, keys_minor=True), st)
    o_ref[0] = _mla_finish(st, wuv_ref, tp)


def _idx_sample_kernel(pt_ref, iq_ref, iw_ref, new_ref, cache_ref, key_ref, buf, sem, *,
                       layer, n_pages, group):
    tp = SAMPLE_TP
    pager = _Pager(pt_ref, cache_ref, buf, sem, layer, n_pages, group, keys_minor=True)
    iq = _idx_queries(iq_ref[0])
    iwb = _idx_weights(iw_ref[0], tp)
    key = _sortable(_idx_score(iq, iwb, new_ref[0], tp))
    s_tot = (n_pages + 1) * PAGE
    key_ref[0, :, pl.ds(n_pages * PAGE, PAGE)] = jnp.where(_sample_vis(1, False), key, INT_MIN)
    if key_ref.shape[2] > s_tot:
        key_ref[0, :, pl.ds(s_tot, key_ref.shape[2] - s_tot)] = jnp.full((tp, key_ref.shape[2] - s_tot), INT_MIN, I32)

    def body(g, rows_ref, c):
        base = pl.multiple_of(pager.first_page(g) * PAGE, PAGE)
        key_ref[0, :, pl.ds(base, group * PAGE)] = _sortable(_idx_score(iq, iwb, rows_ref[...], tp))
        return c

    pager.sweep(body, 0)


def _thr_sample_kernel(key_ref, thr_ref, jl_ref, *, nchunks, cw, n_sel, idx_bits, t_real):
    rows = key_ref.shape[0]
    real = (lax.broadcasted_iota(I32, (rows, 1), 0) & (SAMPLE_TP - 1)) < t_real
    thr, jl = _select_threshold(key_ref, nchunks, cw, n_sel, idx_bits, real)
    thr_ref[...] = jnp.broadcast_to(thr, thr_ref.shape)
    jl_ref[...] = jnp.broadcast_to(jl, jl_ref.shape)


def _dsa_sample_kernel(pt_ref, q_ref, key_ref, thr_ref, jl_ref, new_ref, cache_ref, o_ref, buf, sem, *,
                       layer, n_pages, group):
    tp = SAMPLE_TP
    pager = _Pager(pt_ref, cache_ref, buf, sem, layer, n_pages, group)
    q = (_heads_to_rows(q_ref[0], HEAD_DIM, [h * HEAD_DIM for h in range(N_HEADS)])
         * (HEAD_DIM ** -0.5)).astype(BF16)
    thr = thr_ref[0][:, :1]
    jl = jl_ref[0][:, :1]

    def att(kv, base, n, st):
        bias = _dsa_bias(key_ref[0, :, pl.ds(base, n)], thr, jl, base, N_HEADS)
        s = _dot_nt(q, kv[:, :HEAD_DIM].astype(BF16)) + bias
        return _sm_block(s, kv[:, HEAD_DIM:].astype(BF16), *st)

    st = att(new_ref[0], n_pages * PAGE, PAGE, _sm_init(N_HEADS * tp, HEAD_DIM))

    def body(g, rows_ref, st):
        base = pl.multiple_of(pager.first_page(g) * PAGE, PAGE)
        return att(rows_ref[...], base, group * PAGE, st)

    st = pager.sweep(body, st)
    o_ref[0] = _rows_to_heads(st[2] * (1.0 / st[1]), N_HEADS, tp)


def _paged_call(kern, page_table, pre, cache, out_width, out_dtype, name, keys_minor=False):
    nb = page_table.shape[0]
    specs = []
    for kind, a in pre:
        if kind == "smem":
            specs.append(pl.BlockSpec(memory_space=pltpu.SMEM))
        elif kind == "seq":
            specs.append(pl.BlockSpec((1,) + a.shape[1:], lambda b, pt: (b, 0, 0)))
        else:
            specs.append(pl.BlockSpec(a.shape, lambda b, pt, n=a.ndim: (0,) * n))
    specs.append(pl.BlockSpec(memory_space=pl.ANY))
    pre = [a for _, a in pre]
    group = _largest_divisor(page_table.shape[1], PAGES_PER_GROUP)
    buf_shape = (2, cache.shape[2], group * PAGE) if keys_minor else (2, group * PAGE, cache.shape[3])
    return pl.pallas_call(
        functools.partial(kern, n_pages=page_table.shape[1], group=group),
        grid_spec=pltpu.PrefetchScalarGridSpec(
            num_scalar_prefetch=1, grid=(nb,), in_specs=specs,
            out_specs=pl.BlockSpec((1, SAMPLE_TP, out_width), lambda b, pt: (b, 0, 0)),
            scratch_shapes=[pltpu.VMEM(buf_shape, F32), pltpu.SemaphoreType.DMA((2,))]),
        out_shape=jax.ShapeDtypeStruct((nb, SAMPLE_TP, out_width), out_dtype),
        compiler_params=_cparams("arbitrary"), name=name,
    )(page_table, *pre, cache)


def _pad_rows(x, n):
    return jnp.pad(x, ((0, 0), (0, n - x.shape[1]), (0, 0)))


def sb_sample(page_table, q, new, cache, layer):
    kern = functools.partial(_sb_sample_kernel, layer=layer)
    return _paged_call(kern, page_table, [("seq", q), ("seq", new)], cache, BRANCH_WIDTH, F32, "sb_sample")


def diff_sample(page_table, lam, q, new, g, cache, layer, post_scale):
    kern = functools.partial(_diff_sample_kernel, layer=layer, post_scale=post_scale)
    pre = [("smem", lam), ("seq", q), ("seq", new), ("full", g)]
    return _paged_call(kern, page_table, pre, cache, BRANCH_WIDTH, F32, "diff_sample")


def mla_sample(page_table, ql, qp, new, wuv, cache, layer):
    kern = functools.partial(_mla_sample_kernel, layer=layer)
    pre = [("seq", ql), ("seq", qp), ("seq", new), ("full", wuv)]
    return _paged_call(kern, page_table, pre, cache, BRANCH_WIDTH, F32, "mla_sample", keys_minor=True)


def dsa_sample(page_table, iq, iw, new_idx, cache_idx, q, new_kv, cache_kv, layer, t_real):
    nb, n_pages = page_table.shape
    s_tot = n_pages * PAGE + PAGE
    cw = min(16 * LANES, s_tot)
    s_pad = -(-s_tot // cw) * cw
    kern = functools.partial(_idx_sample_kernel, layer=layer)
    pre = [("seq", iq), ("seq", iw), ("seq", new_idx)]
    keys = _paged_call(kern, page_table, pre, cache_idx, s_pad, I32, "idx_sample", keys_minor=True)
    rows = nb * SAMPLE_TP
    rb = _pick(rows, (64, 32, 16, 8))
    n_sel = min(TOPK_MAX, (n_pages * PAGE + t_real) // 4)
    thr, jl = pl.pallas_call(
        functools.partial(_thr_sample_kernel, nchunks=s_pad // cw, cw=cw, n_sel=n_sel,
                          idx_bits=int(s_tot).bit_length(), t_real=t_real),
        grid=(rows // rb,),
        in_specs=[pl.BlockSpec((rb, s_pad), lambda i: (i, 0))],
        out_specs=[pl.BlockSpec((rb, LANES), lambda i: (i, 0))] * 2,
        out_shape=[jax.ShapeDtypeStruct((rows, LANES), I32)] * 2,
        compiler_params=_cparams("parallel"), name="thr_sample",
    )(keys.reshape(rows, s_pad))
    kern = functools.partial(_dsa_sample_kernel, layer=layer)
    pre = [("seq", q), ("seq", keys), ("seq", thr.reshape(nb, SAMPLE_TP, LANES)),
           ("seq", jl.reshape(nb, SAMPLE_TP, LANES)), ("seq", new_kv)]
    return _paged_call(kern, page_table, pre, cache_kv, BRANCH_WIDTH, F32, "dsa_sample")


def _rms_matmul_kernel(x_ref, g_ref, w_ref, o_ref):
    o_ref[...] = _dot(_rms(x_ref[...], g_ref[...]).astype(BF16), w_ref[...])


def rms_matmul(x, g, w):
    n, d = x.shape
    m = w.shape[1]
    tm = _pick(n, (512, 256, 128, 64, 32, 16, 8))
    tn = _pick(m, (640, 512, 384, 256, 128))
    return pl.pallas_call(
        _rms_matmul_kernel, grid=(n // tm, m // tn),
        in_specs=[pl.BlockSpec((tm, d), lambda i, j: (i, 0)), pl.BlockSpec((1, d), lambda i, j: (0, 0)),
                  pl.BlockSpec((d, tn), lambda i, j: (0, j))],
        out_specs=pl.BlockSpec((tm, tn), lambda i, j: (i, j)),
        out_shape=jax.ShapeDtypeStruct((n, m), F32),
        compiler_params=_cparams("parallel", "arbitrary"), name="rms_matmul",
    )(x, g.reshape(1, d), w)


def _matmul_kernel(x_ref, w_ref, o_ref):
    o_ref[...] = _dot(x_ref[...].astype(BF16), w_ref[...])


def matmul(x, w):
    n, d = x.shape
    m = w.shape[1]
    tm = _pick(n, (512, 256, 128, 64, 32, 16, 8))
    return pl.pallas_call(
        _matmul_kernel, grid=(n // tm,),
        in_specs=[pl.BlockSpec((tm, d), lambda i: (i, 0)), pl.BlockSpec((d, m), lambda i: (0, 0))],
        out_specs=pl.BlockSpec((tm, m), lambda i: (i, 0)),
        out_shape=jax.ShapeDtypeStruct((n, m), F32),
        compiler_params=_cparams("parallel"), name="matmul",
    )(x, w)


def _merge_kernel(h_ref, g_ref, outs_ref, wgate_ref, wbr_ref, wout_ref, o_ref):
    h = h_ref[...]
    d = h.shape[1]
    a = _rms(h, g_ref[...]).astype(BF16)
    outs = outs_ref[...].astype(BF16)
    merged = None
    for n in range(N_BRANCH):
        gate = jax.nn.sigmoid(_dot(a, wgate_ref[:, n * d:(n + 1) * d]))
        y = _dot(outs[:, n * BRANCH_WIDTH:(n + 1) * BRANCH_WIDTH], wbr_ref[n])
        merged = gate * y if merged is None else merged + gate * y
    o_ref[...] = h + _dot(merged.astype(BF16), wout_ref[...])


def merge(h, g, outs, w_gate, w_branch, w_out):
    n, d = h.shape
    tm = _pick(n, (256, 128, 64, 32, 16, 8))
    full = lambda a: pl.BlockSpec(a.shape, lambda i, k=a.ndim: (0,) * k)
    return pl.pallas_call(
        _merge_kernel, grid=(n // tm,),
        in_specs=[pl.BlockSpec((tm, d), lambda i: (i, 0)), pl.BlockSpec((1, d), lambda i: (0, 0)),
                  pl.BlockSpec((tm, outs.shape[1]), lambda i: (i, 0)), full(w_gate), full(w_branch), full(w_out)],
        out_specs=pl.BlockSpec((tm, d), lambda i: (i, 0)),
        out_shape=jax.ShapeDtypeStruct((n, d), F32),
        compiler_params=_cparams("parallel"), name="merge",
    )(h, g.reshape(1, d), outs, w_gate, w_branch, w_out)


def _ffn_kernel(h_ref, g_ref, wg_ref, wu_ref, wd_ref, o_ref, c_scr, acc_scr):
    f = pl.program_id(1)

    @pl.when(f == 0)
    def _():
        c_scr[...] = _rms(h_ref[...], g_ref[...]).astype(BF16)
        acc_scr[...] = jnp.zeros_like(acc_scr)

    c = c_scr[...]
    hid = jax.nn.silu(_dot(c, wg_ref[...])) * _dot(c, wu_ref[...])
    acc_scr[...] += _dot(hid.astype(BF16), wd_ref[...])

    @pl.when(f == pl.num_programs(1) - 1)
    def _():
        o_ref[...] = h_ref[...] + acc_scr[...]


def ffn(h, g, wg, wu, wd):
    n, d = h.shape
    ff = wg.shape[1]
    tm = _pick(n, (1024, 512, 256, 128, 64, 32, 16, 8))
    tf = LANES * _largest_divisor(ff // LANES, 4)
    return pl.pallas_call(
        _ffn_kernel, grid=(n // tm, ff // tf),
        in_specs=[pl.BlockSpec((tm, d), lambda i, f: (i, 0)), pl.BlockSpec((1, d), lambda i, f: (0, 0)),
                  pl.BlockSpec((d, tf), lambda i, f: (0, f)), pl.BlockSpec((d, tf), lambda i, f: (0, f)),
                  pl.BlockSpec((tf, d), lambda i, f: (f, 0))],
        out_specs=pl.BlockSpec((tm, d), lambda i, f: (i, 0)),
        out_shape=jax.ShapeDtypeStruct((n, d), F32),
        scratch_shapes=[pltpu.VMEM((tm, d), BF16), pltpu.VMEM((tm, d), F32)],
        compiler_params=_cparams("parallel", "arbitrary"), name="ffn",
    )(h, g.reshape(1, d), wg, wu, wd)


def _split_bf16(x):
    hi = x.astype(BF16)
    return hi, (x - hi.astype(F32)).astype(BF16)


def _moe_kernel(h_ref, g_ref, wr_hi_ref, wr_lo_ref, wg_ref, wu_ref, wd_ref, o_ref, c_scr, comb_scr, acc_scr):
    e = pl.program_id(1)
    f = pl.program_id(2)
    lane = lax.broadcasted_iota(I32, (1, LANES), 1)

    @pl.when((e == 0) & (f == 0))
    def _():
        c = _rms(h_ref[...], g_ref[...])
        c_scr[...] = c.astype(BF16)
        c_hi, c_lo = _split_bf16(c)
        logits = _dot(c_hi, wr_hi_ref[...]) + (_dot(c_hi, wr_lo_ref[...]) + _dot(c_lo, wr_hi_ref[...]))
        logits = jnp.where(lane < N_EXPERTS, logits, -jnp.inf)
        m1 = jnp.max(logits, axis=1, keepdims=True)
        i1 = jnp.min(jnp.where(logits == m1, lane, LANES), axis=1, keepdims=True)
        rest = jnp.where(lane == i1, -jnp.inf, logits)
        m2 = jnp.max(rest, axis=1, keepdims=True)
        i2 = jnp.min(jnp.where(rest == m2, lane, LANES), axis=1, keepdims=True)
        r = jnp.exp(m2 - m1)
        w1 = 1.0 / (1.0 + r)
        comb_scr[...] = jnp.where(lane == i1, w1, 0.0) + jnp.where(lane == i2, r * w1, 0.0)
        acc_scr[...] = jnp.zeros_like(acc_scr)

    c = c_scr[...]
    ce = jnp.sum(jnp.where(lane == e, comb_scr[...], 0.0), axis=1, keepdims=True)
    hid = jax.nn.silu(_dot(c, wg_ref[...])) * _dot(c, wu_ref[...]) * ce
    acc_scr[...] += _dot(hid.astype(BF16), wd_ref[...])

    @pl.when((e == pl.num_programs(1) - 1) & (f == pl.num_programs(2) - 1))
    def _():
        o_ref[...] = h_ref[...] + acc_scr[...]


def moe(h, g, w_router, wg, wu, wd):
    n, d = h.shape
    ne, _, ff = wg.shape
    tm = _pick(n, (1024, 512, 256, 128, 64, 32, 16, 8))
    tf = LANES * _largest_divisor(ff // LANES, 4)
    wr = jnp.pad(w_router, ((0, 0), (0, LANES - ne)))
    wr_hi = wr.astype(BF16)
    wr_lo = (wr - wr_hi.astype(F32)).astype(BF16)
    return pl.pallas_call(
        _moe_kernel, grid=(n // tm, ne, ff // tf),
        in_specs=[pl.BlockSpec((tm, d), lambda i, e, f: (i, 0)), pl.BlockSpec((1, d), lambda i, e, f: (0, 0)),
                  pl.BlockSpec((d, LANES), lambda i, e, f: (0, 0)), pl.BlockSpec((d, LANES), lambda i, e, f: (0, 0)),
                  pl.BlockSpec((None, d, tf), lambda i, e, f: (e, 0, f)),
                  pl.BlockSpec((None, d, tf), lambda i, e, f: (e, 0, f)),
                  pl.BlockSpec((None, tf, d), lambda i, e, f: (e, f, 0))],
        out_specs=pl.BlockSpec((tm, d), lambda i, e, f: (i, 0)),
        out_shape=jax.ShapeDtypeStruct((n, d), F32),
        scratch_shapes=[pltpu.VMEM((tm, d), BF16), pltpu.VMEM((tm, LANES), F32), pltpu.VMEM((tm, d), F32)],
        compiler_params=_cparams("parallel", "arbitrary", "arbitrary"), name="moe",
    )(h, g.reshape(1, d), wr_hi, wr_lo, wg, wu, wd)


def _ple_kernel(h_ref, g_ref, p_ref, wgate_ref, wple_ref, o_ref):
    h = h_ref[...]
    gate = jax.nn.sigmoid(_dot(_rms(h, g_ref[...]).astype(BF16), wgate_ref[...]))
    o_ref[...] = h + gate * _dot(p_ref[...].astype(BF16), wple_ref[...])


def ple(h, g, p, w_gate, w_ple):
    n, d = h.shape
    tm = _pick(n, (512, 256, 128, 64, 32, 16, 8))
    full = lambda a: pl.BlockSpec(a.shape, lambda i, k=a.ndim: (0,) * k)
    return pl.pallas_call(
        _ple_kernel, grid=(n // tm,),
        in_specs=[pl.BlockSpec((tm, d), lambda i: (i, 0)), pl.BlockSpec((1, d), lambda i: (0, 0)),
                  pl.BlockSpec((tm, p.shape[1]), lambda i: (i, 0)), full(w_gate), full(w_ple)],
        out_specs=pl.BlockSpec((tm, d), lambda i: (i, 0)),
        out_shape=jax.ShapeDtypeStruct((n, d), F32),
        compiler_params=_cparams("parallel"), name="ple",
    )(h, g.reshape(1, d), p, w_gate, w_ple)


def _rmsnorm(x, g):
    return x * lax.rsqrt(jnp.mean(x * x, axis=-1, keepdims=True) + RMS_EPS) * g


def _rope(x, pos):
    half = x.shape[-1] // 2
    inv_freq = ROPE_THETA ** (-jnp.arange(half, dtype=F32) / half)
    ang = pos.astype(F32)[:, None] * inv_freq[None, :]
    ang = ang.reshape((ang.shape[0],) + (1,) * (x.ndim - 2) + (half,))
    cos, sin = jnp.cos(ang), jnp.sin(ang)
    x1, x2 = x[..., :half], x[..., half:]
    return jnp.concatenate([x1 * cos - x2 * sin, x1 * sin + x2 * cos], axis=-1)


def _block_diag_uk(w_uk):
    c, h, e = w_uk.shape
    out = jnp.zeros((h * e, h * c), w_uk.dtype)
    for i in range(h):
        out = out.at[i * e:(i + 1) * e, i * c:(i + 1) * c].set(w_uk[:, i, :].T)
    return out


def _token_features(z, pos, w, i):
    n = z.shape[0]
    h = N_HEADS
    split_at = [int(v) for v in np.cumsum(IN_SPLITS)[:-1]]
    (sb_q, sb_k, sb_v, df_q, df_k, df_v, m_cq, m_ckv, m_kpe,
     ds_q, ds_k, ds_v, ix_q, ix_k, ix_w) = jnp.split(z[:, :IN_COLS], split_at, axis=-1)
    f = {}
    f["sb_q"] = sb_q
    f["sb_row"] = jnp.concatenate([sb_k, sb_v], axis=-1)
    df_q = _rope(_rmsnorm(df_q.reshape(n, h, 2, DIFF_QK), w["diff_q_norm"][i]), pos)
    df_k = _rope(_rmsnorm(df_k.reshape(n, 2, DIFF_QK), w["diff_k_norm"][i]), pos)
    f["df_q"] = df_q.reshape(n, h * 2 * DIFF_QK)
    f["diff_row"] = jnp.concatenate([df_k.reshape(n, 2 * DIFF_QK), df_v], axis=-1)
    w_uq = w["w_mla_uq"][i].reshape(MLA_Q_RANK, h * (MLA_NOPE + MLA_ROPE)).astype(BF16)
    q_full = matmul(_rmsnorm(m_cq, w["mla_q_norm_a"][i]), w_uq).reshape(n, h, MLA_NOPE + MLA_ROPE)
    q_nope = _rmsnorm(q_full[..., :MLA_NOPE], w["mla_qn_nope"][i])
    q_pe = _rope(_rmsnorm(q_full[..., MLA_NOPE:], w["mla_qn_pe"][i]), pos)
    f["q_lat"] = matmul(q_nope.reshape(n, h * MLA_NOPE), _block_diag_uk(w["w_mla_uk"][i]).astype(BF16))
    f["q_pe"] = q_pe.reshape(n, h * MLA_ROPE)
    f["mla_row"] = jnp.concatenate([_rmsnorm(m_ckv, w["mla_kv_norm"][i]),
                                    _rope(_rmsnorm(m_kpe, w["mla_kn_pe"][i]), pos)], axis=-1)
    f["ds_q"] = _rope(_rmsnorm(ds_q.reshape(n, h, HEAD_DIM), w["dsa_q_norm"][i]), pos).reshape(n, h * HEAD_DIM)
    f["ds_row"] = jnp.concatenate([_rope(_rmsnorm(ds_k, w["dsa_k_norm"][i]), pos), ds_v], axis=-1)
    f["ix_q"] = _rope(ix_q.reshape(n, IDX_HEADS, IDX_DIM), pos).reshape(n, IDX_HEADS * IDX_DIM)
    f["idx_row"] = _rope(ix_k, pos)
    f["ix_w"] = ix_w
    return f


def kernel(x_prompt, x_sample, cache_sb, cache_diff, cache_mla, cache_dsa_kv, cache_dsa_idx, page_table, p_prompt, p_sample, norm_mix, w_in, diff_q_norm, diff_k_norm, diff_lq1, diff_lk1, diff_lq2, diff_lk2, diff_subln, mla_q_norm_a, mla_kv_norm, w_mla_uq, w_mla_uk, w_mla_uv, mla_qn_nope, mla_qn_pe, mla_kn_pe, dsa_q_norm, dsa_k_norm, w_branch, w_gate, w_out, norm_ffn, w_ff_gate, w_ff_up, w_ff_down, w_router, w_moe_gate, w_moe_up, w_moe_down, norm_ple, w_ple_gate, w_ple):
    w = dict(diff_q_norm=diff_q_norm, diff_k_norm=diff_k_norm, mla_q_norm_a=mla_q_norm_a,
             mla_kv_norm=mla_kv_norm, w_mla_uq=w_mla_uq, w_mla_uk=w_mla_uk, mla_qn_nope=mla_qn_nope,
             mla_qn_pe=mla_qn_pe, mla_kn_pe=mla_kn_pe, dsa_q_norm=dsa_q_norm, dsa_k_norm=dsa_k_norm)
    bp, tp, d = x_prompt.shape
    bs, ts, _ = x_sample.shape
    depth = w_in.shape[0]
    n_pages = page_table.shape[1]
    past_len = n_pages * PAGE
    npr, nsm = bp * tp, bs * ts
    n_tok = npr + nsm
    n_pad = -n_tok % 256
    pos = jnp.concatenate([jnp.tile(jnp.arange(tp, dtype=I32), bp),
                           jnp.tile(past_len + jnp.arange(ts, dtype=I32), bs),
                           jnp.zeros((n_pad,), I32)])
    h = jnp.concatenate([x_prompt.reshape(npr, d), x_sample.reshape(nsm, d), jnp.zeros((n_pad, d), F32)])
    in_pad = -IN_COLS % LANES
    cache_mla_t = jnp.swapaxes(cache_mla, 2, 3)
    cache_idx_t = jnp.swapaxes(cache_dsa_idx, 2, 3)
    names = ("sb_row", "diff_row", "mla_row", "ds_row", "idx_row")
    new_rows = {k: [] for k in names}

    for i in range(depth):
        z = rms_matmul(h, norm_mix[i], jnp.pad(w_in[i], ((0, 0), (0, in_pad))).astype(BF16))
        f = _token_features(z, pos, w, i)
        pr = lambda a: a[:npr].reshape(bp, tp, a.shape[-1])
        sm = lambda a: a[npr:n_tok].reshape(bs, ts, a.shape[-1])
        smq = lambda a: _pad_rows(sm(a), SAMPLE_TP)
        smk = lambda a: _pad_rows(sm(a), PAGE)
        for k in names:
            new_rows[k].append((pr(f[k]), sm(f[k])))

        lam_init = 0.8 - 0.6 * math.exp(-0.3 * i)
        lam = (jnp.exp(jnp.sum(diff_lq1[i] * diff_lk1[i])) - jnp.exp(jnp.sum(diff_lq2[i] * diff_lk2[i]))
               + lam_init).reshape(1).astype(F32)
        subln = diff_subln[i].reshape(1, HEAD_DIM)
        wuv = jnp.transpose(w_mla_uv[i], (1, 0, 2)).astype(BF16)

        outs_p = [sb_prompt(pr(f["sb_q"]), pr(f["sb_row"])),
                  diff_prompt(lam, pr(f["df_q"]), pr(f["diff_row"]), subln, 1.0 - lam_init),
                  mla_prompt(pr(f["q_lat"]), pr(f["q_pe"]), pr(f["mla_row"]), wuv),
                  dsa_prompt(pr(f["ix_q"]), pr(f["ix_w"]), jnp.swapaxes(pr(f["idx_row"]), 1, 2),
                             pr(f["ds_q"]), pr(f["ds_row"]))]
        smkt = lambda a: jnp.swapaxes(smk(a), 1, 2)
        outs_s = [sb_sample(page_table, smq(f["sb_q"]), smk(f["sb_row"]), cache_sb, i),
                  diff_sample(page_table, lam, smq(f["df_q"]), smk(f["diff_row"]), subln, cache_diff, i,
                              1.0 - lam_init),
                  mla_sample(page_table, smq(f["q_lat"]), smq(f["q_pe"]), smkt(f["mla_row"]), wuv, cache_mla_t, i),
                  dsa_sample(page_table, smq(f["ix_q"]), smq(f["ix_w"]), smkt(f["idx_row"]), cache_idx_t,
                             smq(f["ds_q"]), smk(f["ds_row"]), cache_dsa_kv, i, ts)]
        outs = jnp.concatenate([
            jnp.concatenate([o.reshape(npr, BRANCH_WIDTH) for o in outs_p], axis=1),
            jnp.concatenate([o[:, :ts].reshape(nsm, BRANCH_WIDTH) for o in outs_s], axis=1),
            jnp.zeros((n_pad, N_BRANCH * BRANCH_WIDTH), F32)])
        h = merge(h, norm_mix[i], outs, w_gate[i].astype(BF16), w_branch[i].astype(BF16), w_out[i].astype(BF16))
        j = i // 2
        if i % 2 == 0:
            h = ffn(h, norm_ffn[i], w_ff_gate[j].astype(BF16), w_ff_up[j].astype(BF16), w_ff_down[j].astype(BF16))
        else:
            h = moe(h, norm_ffn[i], w_router[j], w_moe_gate[j].astype(BF16), w_moe_up[j].astype(BF16),
                    w_moe_down[j].astype(BF16))
        p = jnp.concatenate([p_prompt[i].reshape(npr, -1), p_sample[i].reshape(nsm, -1),
                             jnp.zeros((n_pad, p_prompt.shape[-1]), F32)])
        h = ple(h, norm_ple[i], p, w_ple_gate[i].astype(BF16), w_ple[i].astype(BF16))

    y_prompt = h[:npr].reshape(bp, tp, d)
    y_sample = h[npr:n_tok].reshape(bs, ts, d)
    state = []
    for k in names:
        state.append(jnp.stack([r[0] for r in new_rows[k]], axis=0))
        state.append(jnp.stack([r[1] for r in new_rows[k]], axis=0))
    return (y_prompt, y_sample) + tuple(state)
```
